```python
import math
import jax, jax.numpy as jnp
from jax import lax
import numpy as np

D_MODEL = 1024
BATCH = 4
SEQ = 8192
DEPTH = 4

MIX_WIDTH = 2 * D_MODEL
RET_WIDTH = MIX_WIDTH // 4
SSD_WIDTH = MIX_WIDTH // 2
FOX_WIDTH = MIX_WIDTH // 4
RET_HEAD_DIM = 128
RET_HEADS = RET_WIDTH // RET_HEAD_DIM
SSD_HEAD_DIM = 64
SSD_HEADS = SSD_WIDTH // SSD_HEAD_DIM
SSD_GROUPS = 2
SSD_STATE = 128
CONV_K = 4
CONV_DIM = SSD_WIDTH + 2 * SSD_GROUPS * SSD_STATE
FOX_HEAD_DIM = 64
FOX_HEADS = FOX_WIDTH // FOX_HEAD_DIM
CHUNK = 128
Q_BLOCK = 128
ROPE_BASE = 10000.0
EPS = 1e-6

IN_SIZES = (RET_WIDTH, RET_WIDTH, RET_WIDTH, RET_WIDTH,
            CONV_DIM, SSD_HEADS, SSD_WIDTH,
            FOX_WIDTH, FOX_WIDTH, FOX_WIDTH, FOX_WIDTH,
            FOX_HEADS)
N_IN = sum(IN_SIZES)
IN_OFFSETS = tuple(int(v) for v in np.cumsum(IN_SIZES)[:-1])

kernel_name = "hymba_style_retention_ssd_fox_trunk"


def _rmsnorm(t, gain=None):
    t32 = t.astype(jnp.float32)
    t32 = t32 * lax.rsqrt(jnp.mean(t32 * t32, axis=-1, keepdims=True) + EPS)
    if gain is not None:
        t32 = t32 * gain.astype(jnp.float32)
    return t32.astype(t.dtype)


def _rotary(t, positions):
    half = t.shape[-1] // 2
    freq = ROPE_BASE ** (-jnp.arange(half, dtype=jnp.float32) / half)
    ang = positions.astype(jnp.float32)[..., None] * freq
    cos = jnp.cos(ang)[:, :, None, :]
    sin = jnp.sin(ang)[:, :, None, :]
    t1 = t[..., :half].astype(jnp.float32)
    t2 = t[..., half:].astype(jnp.float32)
    return jnp.concatenate([t1 * cos - t2 * sin, t1 * sin + t2 * cos], axis=-1).astype(t.dtype)


def _retention(q, k, v, positions):
    b, L, H, dh = q.shape
    n = L // CHUNK
    q = _rotary(q, positions)
    k = _rotary(k, positions) * (dh ** -0.5)
    log_g = jnp.log(1.0 - 2.0 ** (-5.0 - jnp.arange(H, dtype=jnp.float32)))
    idx = jnp.arange(CHUNK, dtype=jnp.float32)
    diff = idx[:, None] - idx[None, :]
    decay_intra = jnp.where(diff >= 0, jnp.exp(log_g[:, None, None] * jnp.maximum(diff, 0.0)), 0.0)
    decay_q = jnp.exp(log_g[:, None] * (idx + 1.0))
    decay_k = jnp.exp(log_g[:, None] * (CHUNK - 1.0 - idx))
    decay_chunk = jnp.exp(log_g * CHUNK)

    def to_chunks(t):
        return t.reshape(b, n, CHUNK, H, dh).transpose(1, 0, 3, 2, 4)

    def step(S, inp):
        qc, kc, vc = inp
        s = jnp.einsum('bhid,bhjd->bhij', qc, kc) * decay_intra
        o = (jnp.einsum('bhij,bhjd->bhid', s, vc)
             + jnp.einsum('bhid,bhde->bhie', qc, S) * decay_q[:, :, None])
        S = S * decay_chunk[:, None, None] + jnp.einsum('bhjd,hj,bhje->bhde', kc, decay_k, vc)
        return S, o

    S0 = jnp.zeros((b, H, dh, dh), jnp.float32)
    _, o = lax.scan(step, S0, (to_chunks(q), to_chunks(k), to_chunks(v)))
    o = o.transpose(1, 0, 3, 2, 4).reshape(b, L, H, dh)
    return _rmsnorm(o).astype(v.dtype)


def _ssd(xs, dt, A, Bm, Cm):
    b, L, H, P = xs.shape
    G, N = Bm.shape[2], Bm.shape[3]
    hpg = H // G
    n = L // CHUNK
    a = (dt * A).astype(jnp.float32)
    xdt = xs * dt[..., None]
    xc_all = xdt.reshape(b, n, CHUNK, G, hpg, P).transpose(1, 0, 2, 3, 4, 5)
    a_all = a.reshape(b, n, CHUNK, G, hpg).transpose(1, 0, 3, 4, 2)
    b_all = Bm.reshape(b, n, CHUNK, G, N).transpose(1, 0, 2, 3, 4)
    c_all = Cm.reshape(b, n, CHUNK, G, N).transpose(1, 0, 2, 3, 4)
    causal = jnp.tril(jnp.ones((CHUNK, CHUNK), dtype=bool))

    def step(S, inp):
        xc, ac, bc, cc = inp
        acum = jnp.cumsum(ac, axis=-1)
        seg = acum[..., :, None] - acum[..., None, :]
        Lm = jnp.exp(jnp.where(causal, seg, -jnp.inf))
        cb = jnp.einsum('bign,bjgn->bgij', cc, bc)
        y = jnp.einsum('bgij,bghij,bjghp->bighp', cb, Lm, xc)
        y = y + jnp.einsum('bign,bghpn,bghi->bighp', cc, S, jnp.exp(acum))
        last = acum[..., -1:]
        S = (S * jnp.exp(last)[..., None]
             + jnp.einsum('bjgn,bghj,bjghp->bghpn', bc, jnp.exp(last - acum), xc))
        return S, y

    S0 = jnp.zeros((b, G, hpg, P, N), jnp.float32)
    _, y = lax.scan(step, S0, (xc_all, a_all, b_all, c_all))
    return y.transpose(1, 0, 2, 3, 4, 5).reshape(b, L, H, P)


def _causal_dwconv(u, w, bias):
    out = lax.conv_general_dilated(
        u, w[:, None, :].astype(u.dtype), window_strides=(1,), padding=[(CONV_K - 1, 0)],
        dimension_numbers=('NWC', 'WIO', 'NWC'), feature_group_count=u.shape[-1])
    return out + bias


def _forgetting_attention(q, k, v, log_f):
    b, L, H, dh = q.shape
    nb = L // Q_BLOCK
    Ft = jnp.cumsum(log_f.astype(jnp.float32), axis=1).transpose(0, 2, 1)
    q_blocks = q.reshape(b, nb, Q_BLOCK, H, dh).transpose(1, 0, 2, 3, 4)
    f_blocks = Ft.reshape(b, H, nb, Q_BLOCK).transpose(2, 0, 1, 3)
    kpos = jnp.arange(L)
    scale = dh ** -0.5

    def block(inp):
        qi, fi, i = inp
        s = jnp.einsum('bqhd,bkhd->bhqk', qi, k).astype(jnp.float32) * scale
        s = s + fi[..., None] - Ft[:, :, None, :]
        qpos = i * Q_BLOCK + jnp.arange(Q_BLOCK)
        s = jnp.where(qpos[:, None] >= kpos[None, :], s, -jnp.inf)
        p = jax.nn.softmax(s, axis=-1)
        return jnp.einsum('bhqk,bkhd->bqhd', p.astype(v.dtype), v)

    o = lax.map(block, (q_blocks, f_blocks, jnp.arange(nb)))
    return o.transpose(1, 0, 2, 3, 4).reshape(b, L, H, dh)


def _layer(x, c, positions, norm_g, w_ada, b_ada, w_in, conv_w, conv_b,
           dt_bias, a_log, d_skip, ssd_norm_g, b_forget, w_out):
    b, L, _ = x.shape
    mod = jax.nn.silu(c) @ w_ada + b_ada
    shift, scale, gate = jnp.split(mod, 3, axis=-1)
    h = _rmsnorm(x, norm_g) * (1.0 + scale[:, None, :]) + shift[:, None, :]

    proj = h @ w_in
    (rq, rk, rv, rg, xbc, dt_raw, z, fq, fk, fv, fg, f_raw) = jnp.split(proj, IN_OFFSETS, axis=-1)

    ret = _retention(rq.reshape(b, L, RET_HEADS, RET_HEAD_DIM),
                     rk.reshape(b, L, RET_HEADS, RET_HEAD_DIM),
                     rv.reshape(b, L, RET_HEADS, RET_HEAD_DIM), positions)
    ret = ret.reshape(b, L, RET_WIDTH) * jax.nn.silu(rg)

    xbc = jax.nn.silu(_causal_dwconv(xbc, conv_w, conv_b))
    xs, Bm, Cm = jnp.split(xbc, (SSD_WIDTH, SSD_WIDTH + SSD_GROUPS * SSD_STATE), axis=-1)
    xs = xs.reshape(b, L, SSD_HEADS, SSD_HEAD_DIM)
    Bm = Bm.reshape(b, L, SSD_GROUPS, SSD_STATE)
    Cm = Cm.reshape(b, L, SSD_GROUPS, SSD_STATE)
    dt = jax.nn.softplus(dt_raw + dt_bias)
    A = -jnp.exp(a_log.astype(jnp.float32))
    y = _ssd(xs, dt, A, Bm, Cm) + d_skip[:, None] * xs
    ssd = _rmsnorm(y.reshape(b, L, SSD_WIDTH).astype(x.dtype) * jax.nn.silu(z), ssd_norm_g)

    log_f = jax.nn.log_sigmoid((f_raw + b_forget).astype(jnp.float32))
    fox = _forgetting_attention(fq.reshape(b, L, FOX_HEADS, FOX_HEAD_DIM),
                                fk.reshape(b, L, FOX_HEADS, FOX_HEAD_DIM),
                                fv.reshape(b, L, FOX_HEADS, FOX_HEAD_DIM), log_f)
    fox = fox.reshape(b, L, FOX_WIDTH) * jax.nn.silu(fg)

    mixed = jnp.concatenate([ret.astype(x.dtype), ssd.astype(x.dtype), fox.astype(x.dtype)], axis=-1)
    out = mixed @ w_out
    return x + gate[:, None, :] * out


def setup_inputs(seed: int = 0) -> dict:
    key = jax.random.key(seed)
    ks = jax.random.split(key, 16)
    f32 = jnp.float32
    x = jax.random.normal(ks[0], (BATCH, SEQ, D_MODEL), f32)
    c = jax.random.normal(ks[1], (BATCH, D_MODEL), f32)
    offsets = jax.random.randint(ks[2], (BATCH, 1), 0, 4096, dtype=jnp.int32)
    positions = (jnp.arange(SEQ, dtype=jnp.int32)[None, :] + offsets).astype(jnp.int32)
    norm_g = 1.0 + 0.02 * jax.random.normal(ks[3], (DEPTH, D_MODEL), f32)
    w_ada = 0.5 * D_MODEL ** -0.5 * jax.random.normal(ks[4], (DEPTH, D_MODEL, 3 * D_MODEL), f32)
    b_ada = 0.02 * jax.random.normal(ks[5], (DEPTH, 3 * D_MODEL), f32)
    w_in = D_MODEL ** -0.5 * jax.random.normal(ks[6], (DEPTH, D_MODEL, N_IN), f32)
    conv_w = CONV_K ** -0.5 * jax.random.normal(ks[7], (DEPTH, CONV_K, CONV_DIM), f32)
    conv_b = 0.02 * jax.random.normal(ks[8], (DEPTH, CONV_DIM), f32)
    dt0 = jnp.exp(jax.random.uniform(ks[9], (DEPTH, SSD_HEADS), f32, math.log(1e-3), math.log(1e-1)))
    dt_bias = dt0 + jnp.log(-jnp.expm1(-dt0))
    a_log = jnp.log(jax.random.uniform(ks[10], (DEPTH, SSD_HEADS), f32, 1.0, 16.0))
    d_skip = 1.0 + 0.1 * jax.random.normal(ks[11], (DEPTH, SSD_HEADS), f32)
    ssd_norm_g = 1.0 + 0.02 * jax.random.normal(ks[12], (DEPTH, SSD_WIDTH), f32)
    b_forget = jax.random.uniform(ks[13], (DEPTH, FOX_HEADS), f32, 2.0, 5.0)
    w_out = MIX_WIDTH ** -0.5 * jax.random.normal(ks[14], (DEPTH, MIX_WIDTH, D_MODEL), f32)
    final_g = 1.0 + 0.02 * jax.random.normal(ks[15], (D_MODEL,), f32)
    return {"x": x, "c": c, "positions": positions, "norm_g": norm_g, "w_ada": w_ada,
            "b_ada": b_ada, "w_in": w_in, "conv_w": conv_w, "conv_b": conv_b,
            "dt_bias": dt_bias, "a_log": a_log, "d_skip": d_skip, "ssd_norm_g": ssd_norm_g,
            "b_forget": b_forget, "w_out": w_out, "final_g": final_g}


def reference(x, c, positions, norm_g, w_ada, b_ada, w_in, conv_w, conv_b,
              dt_bias, a_log, d_skip, ssd_norm_g, b_forget, w_out, final_g):
    for layer in range(DEPTH):
        x = _layer(x, c, positions, norm_g[layer], w_ada[layer], b_ada[layer], w_in[layer],
                   conv_w[layer], conv_b[layer], dt_bias[layer], a_log[layer], d_skip[layer],
                   ssd_norm_g[layer], b_forget[layer], w_out[layer])
    return _rmsnorm(x, final_g)
```

```python
import functools
import math

import numpy as np
import jax
import jax.numpy as jnp
from jax import lax
from jax.experimental import pallas as pl
from jax.experimental.pallas import tpu as pltpu

F32 = jnp.float32
BF16 = jnp.bfloat16

RET_HEADS = 4
RET_HEAD_DIM = 128
RET_WIDTH = RET_HEADS * RET_HEAD_DIM
SSD_HEADS = 16
SSD_HEAD_DIM = 64
SSD_WIDTH = SSD_HEADS * SSD_HEAD_DIM
SSD_GROUPS = 2
SSD_STATE = 128
SSD_HEADS_PER_GROUP = SSD_HEADS // SSD_GROUPS
CONV_K = 4
BC_WIDTH = SSD_GROUPS * SSD_STATE
CONV_DIM = SSD_WIDTH + 2 * BC_WIDTH
FOX_HEADS = 8
FOX_HEAD_DIM = 64
FOX_WIDTH = FOX_HEADS * FOX_HEAD_DIM
FOX_PAIRS = FOX_HEADS // 2
MIX_WIDTH = RET_WIDTH + SSD_WIDTH + FOX_WIDTH
CHUNK = 128
ROPE_BASE = 10000.0
EPS = 1e-6

COL_RET = 0
COL_XS = 4 * RET_WIDTH
COL_Z = COL_XS + SSD_WIDTH
COL_B = COL_Z + SSD_WIDTH
COL_C = COL_B + BC_WIDTH
COL_FOX = COL_C + BC_WIDTH
N_MAIN = COL_FOX + 4 * FOX_WIDTH
SMALL_W = 128
LANE_F = 0
LANE_DT = FOX_HEADS

LANES = 128
TM_PROJ = 1024
TN_PROJ = 512
TM_OUT = 512
TM_NORM = 1024
TB_RET = 1024
TB_SSD = 512
T_ATT = 512
VMEM_LIMIT = 48 * 1024 * 1024
NEG_BIG = -1e30


def _cparams(sem):
    return pltpu.CompilerParams(dimension_semantics=sem, vmem_limit_bytes=VMEM_LIMIT)


def _dot(a, b):
    return jnp.dot(a, b, preferred_element_type=F32)


def _dot_nt(a, b):
    return lax.dot_general(a, b, (((1,), (1,)), ((), ())), preferred_element_type=F32)


def _dot_tn(a, b):
    return lax.dot_general(a, b, (((0,), (0,)), ((), ())), preferred_element_type=F32)


def _silu(t):
    return t * jax.nn.sigmoid(t)


def _softplus(t):
    return jnp.maximum(t, 0.0) + jnp.log1p(jnp.exp(-jnp.abs(t)))


def _split3(t):
    hi = t.astype(BF16)
    r1 = t - hi.astype(F32)
    mid = r1.astype(BF16)
    lo = (r1 - mid.astype(F32)).astype(BF16)
    return hi, mid, lo


def _modnorm(t, g, scale, shift):
    var = jnp.mean(t * t, axis=-1, keepdims=True)
    return t * lax.rsqrt(var + EPS) * g * (1.0 + scale) + shift


def _mod_kernel(c_ref, w_ref, b_ref, o_ref):
    a = _silu(c_ref[...])
    w = w_ref[0]
    a_hi = a.astype(BF16)
    a_lo = (a - a_hi.astype(F32)).astype(BF16)
    w_hi = w.astype(BF16)
    w_lo = (w - w_hi.astype(F32)).astype(BF16)
    o_ref[0] = _dot(a_hi, w_hi) + _dot(a_hi, w_lo) + _dot(a_lo, w_hi) + b_ref[0]


def _modulation(c, w_ada, b_ada):
    depth, d, n3 = w_ada.shape
    bsz = c.shape[0]
    rows = 16
    c_pad = jnp.zeros((rows, d), F32).at[:bsz].set(c)
    tn = 1024
    out = pl.pallas_call(
        _mod_kernel,
        grid=(depth, n3 // tn),
        in_specs=[pl.BlockSpec((rows, d), lambda l, j: (0, 0)),
                  pl.BlockSpec((1, d, tn), lambda l, j: (l, 0, j)),
                  pl.BlockSpec((1, 1, tn), lambda l, j: (l, 0, j))],
        out_specs=pl.BlockSpec((1, rows, tn), lambda l, j: (l, 0, j)),
        out_shape=jax.ShapeDtypeStruct((depth, rows, n3), F32),
        compiler_params=_cparams(("parallel", "parallel")),
        name="adaln_mod",
    )(c_pad, w_ada, b_ada.reshape(depth, 1, n3))
    return out[:, :bsz]


def _rope_kernel(pos_ref, freq_ref, cos_ref, sin_ref):
    pos = pos_ref[0].astype(F32)
    ang = freq_ref[...] * pos
    c = jnp.cos(ang)
    s = jnp.sin(ang)
    cos_ref[...] = jnp.concatenate([c, c], axis=0).T
    sin_ref[...] = jnp.concatenate([-s, s], axis=0).T


def _rope_tables(positions):
    bsz, seq = positions.shape
    half = RET_HEAD_DIM // 2
    freq = (ROPE_BASE ** (-jnp.arange(half, dtype=F32) / half)).reshape(half, 1)
    tm = 512
    nt = seq // tm
    shp = jax.ShapeDtypeStruct((bsz * seq, RET_HEAD_DIM), F32)
    return pl.pallas_call(
        _rope_kernel,
        grid=(bsz, nt),
        in_specs=[pl.BlockSpec((1, 1, tm), lambda b, i: (b, 0, i)),
                  pl.BlockSpec((half, 1), lambda b, i: (0, 0))],
        out_specs=[pl.BlockSpec((tm, RET_HEAD_DIM), lambda b, i: (b * nt + i, 0)),
                   pl.BlockSpec((tm, RET_HEAD_DIM), lambda b, i: (b * nt + i, 0))],
        out_shape=[shp, shp],
        compiler_params=_cparams(("parallel", "parallel")),
        name="rope_tables",
    )(positions.reshape(bsz, 1, seq), freq)


def _norm_kernel(x_ref, g_ref, sc_ref, sh_ref, h_ref):
    h_ref[...] = _modnorm(x_ref[...], g_ref[...], sc_ref[0], sh_ref[0]).astype(h_ref.dtype)


def _first_norm(x2, g, scale, shift, seq):
    t, d = x2.shape
    tm = min(TM_NORM, seq)
    per_b = seq // tm
    return pl.pallas_call(
        _norm_kernel,
        grid=(t // tm,),
        in_specs=[pl.BlockSpec((tm, d), lambda i: (i, 0)),
                  pl.BlockSpec((1, d), lambda i: (0, 0)),
                  pl.BlockSpec((1, 1, d), lambda i: (i // per_b, 0, 0)),
                  pl.BlockSpec((1, 1, d), lambda i: (i // per_b, 0, 0))],
        out_specs=pl.BlockSpec((tm, d), lambda i: (i, 0)),
        out_shape=jax.ShapeDtypeStruct((t, d), BF16),
        compiler_params=_cparams(("parallel",)),
        name="first_norm",
    )(x2, g.reshape(1, d), scale, shift)


def _inproj_kernel(h_ref, w_ref, ws_ref, o_ref, os_ref):
    h = h_ref[...]
    o_ref[...] = _dot(h, w_ref[...]).astype(o_ref.dtype)

    @pl.when(pl.program_id(1) == 0)
    def _():
        os_ref[...] = _dot(h, ws_ref[...])


def _in_projection(h, w_main, w_small):
    t, d = h.shape
    tm = min(TM_PROJ, t)
    return pl.pallas_call(
        _inproj_kernel,
        grid=(t // tm, N_MAIN // TN_PROJ),
        in_specs=[pl.BlockSpec((tm, d), lambda i, j: (i, 0)),
                  pl.BlockSpec((d, TN_PROJ), lambda i, j: (0, j)),
                  pl.BlockSpec((d, SMALL_W), lambda i, j: (0, 0))],
        out_specs=[pl.BlockSpec((tm, TN_PROJ), lambda i, j: (i, j)),
                   pl.BlockSpec((tm, SMALL_W), lambda i, j: (i, 0))],
        out_shape=[jax.ShapeDtypeStruct((t, N_MAIN), BF16),
                   jax.ShapeDtypeStruct((t, SMALL_W), F32)],
        compiler_params=_cparams(("parallel", "arbitrary")),
        name="in_proj",
    )(h, w_main, w_small)


def _retention_consts():
    h = np.arange(RET_HEADS, dtype=np.float64)
    log_g = np.log(1.0 - 2.0 ** (-5.0 - h))
    idx = np.arange(CHUNK, dtype=np.float64)
    diff = idx[:, None] - idx[None, :]
    intra = np.where(diff >= 0, np.exp(log_g[:, None, None] * np.maximum(diff, 0.0)), 0.0)
    dq = np.exp(log_g[:, None] * (idx + 1.0))
    dk = np.exp(log_g[:, None] * (CHUNK - 1.0 - idx))
    dq = np.broadcast_to(dq[:, :, None], (RET_HEADS, CHUNK, RET_HEAD_DIM))
    dk = np.broadcast_to(dk[:, :, None], (RET_HEADS, CHUNK, RET_HEAD_DIM))
    dchunk = tuple(float(v) for v in np.exp(log_g * CHUNK))
    return (jnp.asarray(intra, F32), jnp.asarray(dq, F32), jnp.asarray(dk, F32), dchunk)


def _ret_kernel(q_ref, k_ref, v_ref, g_ref, cos_ref, sin_ref, di_ref, dq_ref, dk_ref,
                o_ref, s_ref, *, nchunk, dchunk):
    @pl.when(pl.program_id(1) == 0)
    def _():
        s_ref[...] = jnp.zeros_like(s_ref)

    kscale = RET_HEAD_DIM ** -0.5
    for h in range(RET_HEADS):
        cols = slice(h * RET_HEAD_DIM, (h + 1) * RET_HEAD_DIM)

        def body(c, carry, h=h, cols=cols):
            rows = pl.ds(pl.multiple_of(c * CHUNK, CHUNK), CHUNK)
            cs = cos_ref[rows, :]
            sn = sin_ref[rows, :]
            q = q_ref[rows, cols].astype(F32)
            k = k_ref[rows, cols].astype(F32)
            v = v_ref[rows, cols]
            qr = q * cs + pltpu.roll(q, RET_HEAD_DIM // 2, 1) * sn
            kr = (k * cs + pltpu.roll(k, RET_HEAD_DIM // 2, 1) * sn) * kscale
            qb = qr.astype(BF16)
            kb = kr.astype(BF16)
            state = s_ref[h]
            s = _dot_nt(qb, kb) * di_ref[h]
            o = _dot(s.astype(BF16), v) + _dot(qb, state.astype(BF16)) * dq_ref[h]
            kd = (kr * dk_ref[h]).astype(BF16)
            s_ref[h] = state * dchunk[h] + _dot_tn(kd, v)
            o = o * lax.rsqrt(jnp.mean(o * o, axis=-1, keepdims=True) + EPS)
            o = o * _silu(g_ref[rows, cols].astype(F32))
            o_ref[rows, cols] = o.astype(o_ref.dtype)
            return carry

        lax.fori_loop(0, nchunk, body, 0)


def _retention(proj, cos_t, sin_t, consts, bsz, seq):
    t = proj.shape[0]
    tb = min(TB_RET, seq)
    per_b = seq // tb
    intra, dq, dk, dchunk = consts
    row = lambda b, j: b * per_b + j
    cblk = lambda idx: pl.BlockSpec((tb, RET_WIDTH), lambda b, j: (row(b, j), idx))
    const3 = pl.BlockSpec((RET_HEADS, CHUNK, RET_HEAD_DIM), lambda b, j: (0, 0, 0))
    tblk = pl.BlockSpec((tb, RET_HEAD_DIM), lambda b, j: (row(b, j), 0))
    base = COL_RET // RET_WIDTH
    return pl.pallas_call(
        functools.partial(_ret_kernel, nchunk=tb // CHUNK, dchunk=dchunk),
        grid=(bsz, per_b),
        in_specs=[cblk(base), cblk(base + 1), cblk(base + 2), cblk(base + 3),
                  tblk, tblk, const3, const3, const3],
        out_specs=pl.BlockSpec((tb, RET_WIDTH), lambda b, j: (row(b, j), 0)),
        out_shape=jax.ShapeDtypeStruct((t, RET_WIDTH), BF16),
        scratch_shapes=[pltpu.VMEM((RET_HEADS, RET_HEAD_DIM, RET_HEAD_DIM), F32)],
        compiler_params=_cparams(("parallel", "arbitrary")),
        name="retention",
    )(proj, proj, proj, proj, cos_t, sin_t, intra, dq, dk)


def _ssd_kernel(xs_ref, z_ref, b_ref, c_ref, sm_ref, cw_ref, cb_ref, bias_ref, alog_ref,
                dskip_ref, ng_ref, tril_ref, y_ref, f_ref,
                ubuf, st_ref, ybuf, fcar, *, tb):
    j = pl.program_id(1)
    tail = CONV_K - 1
    pad = 8

    @pl.when(j == 0)
    def _():
        ubuf[0:pad, :] = jnp.zeros((pad, CONV_DIM), F32)
        st_ref[...] = jnp.zeros_like(st_ref)
        fcar[...] = jnp.zeros_like(fcar)

    @pl.when(j > 0)
    def _():
        ubuf[0:pad, :] = ubuf[tb:tb + pad, :]

    ubuf[pad:pad + tb, 0:SSD_WIDTH] = xs_ref[...].astype(F32)
    ubuf[pad:pad + tb, SSD_WIDTH:SSD_WIDTH + BC_WIDTH] = b_ref[...].astype(F32)
    ubuf[pad:pad + tb, SSD_WIDTH + BC_WIDTH:CONV_DIM] = c_ref[...].astype(F32)

    lane = lax.broadcasted_iota(jnp.int32, (1, LANES), 1)
    is_f = lane < LANE_DT
    half_lane = lane < SSD_HEAD_DIM
    a_row = jnp.where((lane >= LANE_DT) & (lane < LANE_DT + SSD_HEADS), -jnp.exp(alog_ref[...]), 0.0)
    ri = lax.broadcasted_iota(jnp.int32, (CHUNK, CHUNK), 0)
    ci = lax.broadcasted_iota(jnp.int32, (CHUNK, CHUNK), 1)
    causal = ri >= ci
    tril = tril_ref[...]

    for c in range(tb // CHUNK):
        r0 = c * CHUNK
        acc = cb_ref[...] + cw_ref[0:1, :] * ubuf[pad + r0 - tail:pad + r0 - tail + CHUNK, :]
        for kk in range(1, CONV_K):
            acc = acc + cw_ref[kk:kk + 1, :] * ubuf[pad + r0 - tail + kk:pad + r0 - tail + kk + CHUNK, :]
        xc = _silu(acc)
        xs = xc[:, 0:SSD_WIDTH]
        xs_b = xs.astype(BF16)

        pre = sm_ref[r0:r0 + CHUNK, :] + bias_ref[...]
        sp = _softplus(pre)
        val = jnp.where(is_f, -_softplus(-pre), sp * a_row)
        v_hi, v_mid, v_lo = _split3(val)
        cum = _dot(tril, v_hi) + _dot(tril, v_mid) + _dot(tril, v_lo)
        cum = cum + fcar[...]
        fcar[...] = jnp.where(is_f, cum[CHUNK - 1:CHUNK, :], 0.0)
        cum_t = cum.T
        dt_t = sp.T
        f_ref[0, :, r0:r0 + CHUNK] = cum_t[LANE_F:LANE_F + FOX_HEADS, :]
        e_col = jnp.exp(cum)

        for g in range(SSD_GROUPS):
            bm = xc[:, SSD_WIDTH + g * SSD_STATE:SSD_WIDTH + (g + 1) * SSD_STATE]
            cm = xc[:, SSD_WIDTH + BC_WIDTH + g * SSD_STATE:SSD_WIDTH + BC_WIDTH + (g + 1) * SSD_STATE]
            cm_b = cm.astype(BF16)
            cb = _dot_nt(cm_b, bm.astype(BF16))
            bm_t = bm.T
            st_old = st_ref[g]
            inter = _dot(cm_b, st_old.astype(BF16))
            for pr in range(SSD_HEADS_PER_GROUP // 2):
                gp = g * (SSD_HEADS_PER_GROUP // 2) + pr
                xp_b = xs_b[:, gp * LANES:(gp + 1) * LANES]
                ys = []
                ups = []
                ecs = []
                els = []
                for hh in range(2):
                    ln = LANE_DT + 2 * gp + hh
                    a_c = cum[:, ln:ln + 1]
                    a_r = cum_t[ln:ln + 1, :]
                    d_r = dt_t[ln:ln + 1, :]
                    lm = jnp.where(causal, jnp.exp(a_c - a_r), 0.0)
                    m = cb * lm * d_r
                    ys.append(_dot(m.astype(BF16), xp_b))
                    last = a_r[:, CHUNK - 1:CHUNK]
                    w_r = d_r * jnp.exp(last - a_r)
                    ups.append(_dot((bm_t * w_r).astype(BF16), xp_b))
                    ecs.append(e_col[:, ln:ln + 1])
                    els.append(jnp.exp(last))
                lanes = slice(pr * LANES, (pr + 1) * LANES)
                y_pair = (jnp.where(half_lane, ys[0], ys[1])
                          + jnp.where(half_lane, ecs[0], ecs[1]) * inter[:, lanes])
                ybuf[:, gp * LANES:(gp + 1) * LANES] = y_pair
                st_ref[g, :, lanes] = (st_old[:, lanes] * jnp.where(half_lane, els[0], els[1])
                                       + jnp.where(half_lane, ups[0], ups[1]))

        y = ybuf[...] + dskip_ref[...] * xs
        y = y * _silu(z_ref[r0:r0 + CHUNK, :].astype(F32))
        y = y * lax.rsqrt(jnp.mean(y * y, axis=-1, keepdims=True) + EPS) * ng_ref[...]
        y_ref[r0:r0 + CHUNK, :] = y.astype(y_ref.dtype)


def _ssd(proj, small, conv_w, conv_b, bias_row, alog_row, dskip_row, norm_g, tril, bsz, seq):
    t = proj.shape[0]
    tb = min(TB_SSD, seq)
    per_b = seq // tb
    row = lambda b, j: b * per_b + j
    full = lambda shape: pl.BlockSpec(shape, lambda b, j: (0,) * len(shape))
    return pl.pallas_call(
        functools.partial(_ssd_kernel, tb=tb),
        grid=(bsz, per_b),
        in_specs=[pl.BlockSpec((tb, SSD_WIDTH), lambda b, j: (row(b, j), COL_XS // SSD_WIDTH)),
                  pl.BlockSpec((tb, SSD_WIDTH), lambda b, j: (row(b, j), COL_Z // SSD_WIDTH)),
                  pl.BlockSpec((tb, BC_WIDTH), lambda b, j: (row(b, j), COL_B // BC_WIDTH)),
                  pl.BlockSpec((tb, BC_WIDTH), lambda b, j: (row(b, j), COL_C // BC_WIDTH)),
                  pl.BlockSpec((tb, SMALL_W), lambda b, j: (row(b, j), 0)),
                  full((CONV_K, CONV_DIM)), full((1, CONV_DIM)), full((1, SMALL_W)),
                  full((1, SMALL_W)), full((1, SSD_WIDTH)), full((1, SSD_WIDTH)),
                  full((CHUNK, CHUNK))],
        out_specs=[pl.BlockSpec((tb, SSD_WIDTH), lambda b, j: (row(b, j), 0)),
                   pl.BlockSpec((1, FOX_HEADS, tb), lambda b, j: (b, 0, j))],
        out_shape=[jax.ShapeDtypeStruct((t, SSD_WIDTH), BF16),
                   jax.ShapeDtypeStruct((bsz, FOX_HEADS, seq), F32)],
        scratch_shapes=[pltpu.VMEM((tb + 8, CONV_DIM), F32),
                        pltpu.VMEM((SSD_GROUPS, SSD_STATE, SSD_HEADS_PER_GROUP * SSD_HEAD_DIM), F32),
                        pltpu.VMEM((CHUNK, SSD_WIDTH), F32),
                        pltpu.VMEM((1, SMALL_W), F32)],
        compiler_params=_cparams(("parallel", "arbitrary")),
        name="ssd",
    )(proj, proj, proj, proj, small, conv_w, conv_b, bias_row, alog_row, dskip_row, norm_g, tril)


def _fox_kernel(q_ref, k_ref, v_ref, g_ref, f_ref, o_ref, m_ref, l_ref, a_ref, *, tq):
    i = pl.program_id(2)
    lane = lax.broadcasted_iota(jnp.int32, (1, LANES), 1)
    head0 = lane < FOX_HEAD_DIM
    q = q_ref[...] * (FOX_HEAD_DIM ** -0.5)
    zero = jnp.zeros_like(q)
    qh = (jnp.where(head0, q, zero), jnp.where(head0, zero, q))

    m_ref[...] = jnp.full(m_ref.shape, NEG_BIG, F32)
    l_ref[...] = jnp.zeros_like(l_ref)
    a_ref[...] = jnp.zeros_like(a_ref)

    def block(kb, masked):
        cols = pl.ds(pl.multiple_of(kb * tq, tq), tq)
        k = k_ref[cols, :]
        v = v_ref[cols, :]
        f = f_ref[0, 0, :, cols]
        if masked:
            ri = lax.broadcasted_iota(jnp.int32, (tq, tq), 0)
            ci = lax.broadcasted_iota(jnp.int32, (tq, tq), 1)
            keep = ri >= ci
        for h in range(2):
            s = _dot_nt(qh[h], k) - f[h:h + 1, :]
            if masked:
                s = jnp.where(keep, s, NEG_BIG)
            m_old = m_ref[h]
            m_new = jnp.maximum(m_old, jnp.max(s, axis=-1, keepdims=True))
            alpha = jnp.exp(m_old - m_new)
            p = jnp.exp(s - m_new)
            l_ref[h] = alpha * l_ref[h] + jnp.sum(p, axis=-1, keepdims=True)
            a_ref[h] = alpha * a_ref[h] + _dot(p.astype(BF16), v)
            m_ref[h] = m_new

    def body(kb, carry):
        block(kb, False)
        return carry

    lax.fori_loop(0, i, body, 0)
    block(i, True)

    o = jnp.where(head0, a_ref[0] / l_ref[0], a_ref[1] / l_ref[1])
    o = o * _silu(g_ref[...].astype(F32))
    o_ref[...] = o.astype(o_ref.dtype)


def _fox(proj, f_rows, bsz, seq):
    t = proj.shape[0]
    tq = min(T_ATT, seq)
    nq = seq // tq
    base = COL_FOX // LANES
    nblk = FOX_WIDTH // LANES
    f4 = f_rows.reshape(bsz, FOX_PAIRS, 2, seq)
    return pl.pallas_call(
        functools.partial(_fox_kernel, tq=tq),
        grid=(bsz, FOX_PAIRS, nq),
        in_specs=[pl.BlockSpec((tq, LANES), lambda b, p, i: (b * nq + i, base + p)),
                  pl.BlockSpec((seq, LANES), lambda b, p, i: (b, base + nblk + p)),
                  pl.BlockSpec((seq, LANES), lambda b, p, i: (b, base + 2 * nblk + p)),
                  pl.BlockSpec((tq, LANES), lambda b, p, i: (b * nq + i, base + 3 * nblk + p)),
                  pl.BlockSpec((1, 1, 2, seq), lambda b, p, i: (b, p, 0, 0))],
        out_specs=pl.BlockSpec((tq, LANES), lambda b, p, i: (b * nq + i, p)),
        out_shape=jax.ShapeDtypeStruct((t, FOX_WIDTH), BF16),
        scratch_shapes=[pltpu.VMEM((2, tq, 1), F32),
                        pltpu.VMEM((2, tq, 1), F32),
                        pltpu.VMEM((2, tq, LANES), F32)],
        compiler_params=_cparams(("parallel", "parallel", "arbitrary")),
        name="fox_attention",
    )(proj, proj, proj, proj, f4)


def _out_kernel(ret_ref, ssd_ref, fox_ref, w_ref, x_ref, gate_ref, g_ref, sc_ref, sh_ref,
                *out_refs, emit_x):
    acc = _dot(ret_ref[...], w_ref[0:RET_WIDTH, :])
    acc = acc + _dot(ssd_ref[...], w_ref[RET_WIDTH:RET_WIDTH + SSD_WIDTH, :])
    acc = acc + _dot(fox_ref[...], w_ref[RET_WIDTH + SSD_WIDTH:MIX_WIDTH, :])
    xn = x_ref[...] + gate_ref[0] * acc
    if emit_x:
        out_refs[0][...] = xn
    y_ref = out_refs[-1]
    y_ref[...] = _modnorm(xn, g_ref[...], sc_ref[0], sh_ref[0]).astype(y_ref.dtype)


def _out_projection(ret, ssd, fox, w_out, x2, gate, g, scale, shift, seq, y_dtype, emit_x):
    t, d = x2.shape
    tm = min(TM_OUT, seq)
    per_b = seq // tm
    rowblk = lambda w: pl.BlockSpec((tm, w), lambda i: (i, 0))
    perb = pl.BlockSpec((1, 1, d), lambda i: (i // per_b, 0, 0))
    out_specs = [rowblk(d)]
    out_shape = [jax.ShapeDtypeStruct((t, d), y_dtype)]
    if emit_x:
        out_specs = [rowblk(d)] + out_specs
        out_shape = [jax.ShapeDtypeStruct((t, d), F32)] + out_shape
    return pl.pallas_call(
        functools.partial(_out_kernel, emit_x=emit_x),
        grid=(t // tm,),
        in_specs=[rowblk(RET_WIDTH), rowblk(SSD_WIDTH), rowblk(FOX_WIDTH),
                  pl.BlockSpec((MIX_WIDTH, d), lambda i: (0, 0)),
                  rowblk(d), perb, pl.BlockSpec((1, d), lambda i: (0, 0)), perb, perb],
        out_specs=out_specs,
        out_shape=out_shape,
        compiler_params=_cparams(("parallel",)),
        name="out_proj",
    )(ret, ssd, fox, w_out, x2, gate, g.reshape(1, d), scale, shift)


def _permute_in_weights(w_in):
    o_xbc = 4 * RET_WIDTH
    o_dt = o_xbc + CONV_DIM
    o_z = o_dt + SSD_HEADS
    o_fox = o_z + SSD_WIDTH
    o_f = o_fox + 4 * FOX_WIDTH
    main = jnp.concatenate([
        w_in[..., 0:o_xbc],
        w_in[..., o_xbc:o_xbc + SSD_WIDTH],
        w_in[..., o_z:o_z + SSD_WIDTH],
        w_in[..., o_xbc + SSD_WIDTH:o_dt],
        w_in[..., o_fox:o_f],
    ], axis=-1).astype(BF16)
    pad = jnp.zeros(w_in.shape[:-1] + (SMALL_W - FOX_HEADS - SSD_HEADS,), w_in.dtype)
    small = jnp.concatenate([w_in[..., o_f:o_f + FOX_HEADS], w_in[..., o_dt:o_z], pad],
                            axis=-1).astype(BF16)
    return main, small


def _lane_row(depth, pieces):
    row = jnp.zeros((depth, 1, SMALL_W), F32)
    for start, arr in pieces:
        row = row.at[:, 0, start:start + arr.shape[-1]].set(arr.astype(F32))
    return row


def kernel(x, c, positions, norm_g, w_ada, b_ada, w_in, conv_w, conv_b, dt_bias, a_log, d_skip,
           ssd_norm_g, b_forget, w_out, final_g):
    bsz, seq, d = x.shape
    depth = w_in.shape[0]
    t = bsz * seq

    mod = _modulation(c, w_ada, b_ada)
    shift = mod[:, :, 0:d].reshape(depth, bsz, 1, d)
    scale = mod[:, :, d:2 * d].reshape(depth, bsz, 1, d)
    gate = mod[:, :, 2 * d:3 * d].reshape(depth, bsz, 1, d)

    cos_t, sin_t = _rope_tables(positions)
    w_main, w_small = _permute_in_weights(w_in)
    w_out_b = w_out.astype(BF16)
    ret_consts = _retention_consts()
    tril = jnp.asarray(np.tril(np.ones((CHUNK, CHUNK), np.float32)), BF16)
    bias_rows = _lane_row(depth, [(LANE_F, b_forget), (LANE_DT, dt_bias)])
    alog_rows = _lane_row(depth, [(LANE_DT, a_log)])
    dskip_rows = jnp.repeat(d_skip, SSD_HEAD_DIM, axis=-1).reshape(depth, 1, SSD_WIDTH)
    zeros_bd = jnp.zeros((bsz, 1, d), F32)

    x2 = x.reshape(t, d)
    h = _first_norm(x2, norm_g[0], scale[0], shift[0], seq)
    out = None
    for l in range(depth):
        proj, small = _in_projection(h, w_main[l], w_small[l])
        ret = _retention(proj, cos_t, sin_t, ret_consts, bsz, seq)
        ssd, f_rows = _ssd(proj, small, conv_w[l], conv_b[l].reshape(1, CONV_DIM), bias_rows[l],
                           alog_rows[l], dskip_rows[l], ssd_norm_g[l].reshape(1, SSD_WIDTH), tril,
                           bsz, seq)
        fox = _fox(proj, f_rows, bsz, seq)
        if l + 1 < depth:
            x2, h = _out_projection(ret, ssd, fox, w_out_b[l], x2, gate[l], norm_g[l + 1],
                                    scale[l + 1], shift[l + 1], seq, BF16, True)
        else:
            (out,) = _out_projection(ret, ssd, fox, w_out_b[l], x2, gate[l], final_g,
                                     zeros_bd, zeros_bd, seq, F32, False)
    return out.reshape(bsz, seq, d)
```

```python
import functools
import math

import numpy as np
import jax
import jax.numpy as jnp
from jax import lax
from jax.experimental import pallas as pl
from jax.experimental.pallas import tpu as pltpu

F32 = jnp.float32
BF16 = jnp.bfloat16

RET_HEADS = 4
RET_HEAD_DIM = 128
RET_WIDTH = RET_HEADS * RET_HEAD_DIM
SSD_HEADS = 16
SSD_HEAD_DIM = 64
SSD_WIDTH = SSD_HEADS * SSD_HEAD_DIM
SSD_GROUPS = 2
SSD_STATE = 128
SSD_HEADS_PER_GROUP = SSD_HEADS // SSD_GROUPS
CONV_K = 4
BC_WIDTH = SSD_GROUPS * SSD_STATE
CONV_DIM = SSD_WIDTH + 2 * BC_WIDTH
FOX_HEADS = 8
FOX_HEAD_DIM = 64
FOX_WIDTH = FOX_HEADS * FOX_HEAD_DIM
FOX_PAIRS = FOX_HEADS // 2
MIX_WIDTH = RET_WIDTH + SSD_WIDTH + FOX_WIDTH
CHUNK = 128
ROPE_BASE = 10000.0
EPS = 1e-6

COL_RET = 0
COL_XS = 4 * RET_WIDTH
COL_Z = COL_XS + SSD_WIDTH
COL_B = COL_Z + SSD_WIDTH
COL_C = COL_B + BC_WIDTH
COL_FOX = COL_C + BC_WIDTH
N_MAIN = COL_FOX + 4 * FOX_WIDTH
SMALL_W = 128
LANE_F = 0
LANE_DT = FOX_HEADS

LANES = 128
TM_PROJ = 1024
TN_PROJ = 512
TM_OUT = 512
TM_NORM = 1024
TB_RET = 1024
TB_SSD = 512
T_ATT = 512
VMEM_LIMIT = 48 * 1024 * 1024
NEG_BIG = -1e30


def _cparams(sem):
    return pltpu.CompilerParams(dimension_semantics=sem, vmem_limit_bytes=VMEM_LIMIT)


def _dot(a, b):
    return jnp.dot(a, b, preferred_element_type=F32)


def _dot_nt(a, b):
    return lax.dot_general(a, b, (((1,), (1,)), ((), ())), preferred_element_type=F32)


def _dot_tn(a, b):
    return lax.dot_general(a, b, (((0,), (0,)), ((), ())), preferred_element_type=F32)


def _silu(t):
    return t * jax.nn.sigmoid(t)


def _softplus(t):
    return jnp.maximum(t, 0.0) + jnp.log1p(jnp.exp(-jnp.abs(t)))


def _split3(t):
    hi = t.astype(BF16)
    r1 = t - hi.astype(F32)
    mid = r1.astype(BF16)
    lo = (r1 - mid.astype(F32)).astype(BF16)
    return hi, mid, lo


def _modnorm(t, g, scale, shift):
    var = jnp.mean(t * t, axis=-1, keepdims=True)
    return t * lax.rsqrt(var + EPS) * g * (1.0 + scale) + shift


def _mod_kernel(c_ref, w_ref, b_ref, o_ref):
    a = _silu(c_ref[...])
    w = w_ref[0]
    a_hi = a.astype(BF16)
    a_lo = (a - a_hi.astype(F32)).astype(BF16)
    w_hi = w.astype(BF16)
    w_lo = (w - w_hi.astype(F32)).astype(BF16)
    o_ref[0] = _dot(a_hi, w_hi) + _dot(a_hi, w_lo) + _dot(a_lo, w_hi) + b_ref[0]


def _modulation(c, w_ada, b_ada):
    depth, d, n3 = w_ada.shape
    bsz = c.shape[0]
    rows = 16
    c_pad = jnp.zeros((rows, d), F32).at[:bsz].set(c)
    tn = 1024
    out = pl.pallas_call(
        _mod_kernel,
        grid=(depth, n3 // tn),
        in_specs=[pl.BlockSpec((rows, d), lambda l, j: (0, 0)),
                  pl.BlockSpec((1, d, tn), lambda l, j: (l, 0, j)),
                  pl.BlockSpec((1, 1, tn), lambda l, j: (l, 0, j))],
        out_specs=pl.BlockSpec((1, rows, tn), lambda l, j: (l, 0, j)),
        out_shape=jax.ShapeDtypeStruct((depth, rows, n3), F32),
        compiler_params=_cparams(("parallel", "parallel")),
        name="adaln_mod",
    )(c_pad, w_ada, b_ada.reshape(depth, 1, n3))
    return out[:, :bsz]


def _rope_kernel(pos_ref, freq_ref, cos_ref, sin_ref):
    pos = pos_ref[0].astype(F32)
    ang = freq_ref[...] * pos
    c = jnp.cos(ang)
    s = jnp.sin(ang)
    cos_ref[...] = jnp.concatenate([c, c], axis=0).T
    sin_ref[...] = jnp.concatenate([-s, s], axis=0).T


def _rope_tables(positions):
    bsz, seq = positions.shape
    half = RET_HEAD_DIM // 2
    freq = (ROPE_BASE ** (-jnp.arange(half, dtype=F32) / half)).reshape(half, 1)
    tm = 512
    nt = seq // tm
    shp = jax.ShapeDtypeStruct((bsz * seq, RET_HEAD_DIM), F32)
    return pl.pallas_call(
        _rope_kernel,
        grid=(bsz, nt),
        in_specs=[pl.BlockSpec((1, 1, tm), lambda b, i: (b, 0, i)),
                  pl.BlockSpec((half, 1), lambda b, i: (0, 0))],
        out_specs=[pl.BlockSpec((tm, RET_HEAD_DIM), lambda b, i: (b * nt + i, 0)),
                   pl.BlockSpec((tm, RET_HEAD_DIM), lambda b, i: (b * nt + i, 0))],
        out_shape=[shp, shp],
        compiler_params=_cparams(("parallel", "parallel")),
        name="rope_tables",
    )(positions.reshape(bsz, 1, seq), freq)


def _norm_kernel(x_ref, g_ref, sc_ref, sh_ref, h_ref):
    h_ref[...] = _modnorm(x_ref[...], g_ref[...], sc_ref[0], sh_ref[0]).astype(h_ref.dtype)


def _first_norm(x2, g, scale, shift, seq):
    t, d = x2.shape
    tm = min(TM_NORM, seq)
    per_b = seq // tm
    return pl.pallas_call(
        _norm_kernel,
        grid=(t // tm,),
        in_specs=[pl.BlockSpec((tm, d), lambda i: (i, 0)),
                  pl.BlockSpec((1, d), lambda i: (0, 0)),
                  pl.BlockSpec((1, 1, d), lambda i: (i // per_b, 0, 0)),
                  pl.BlockSpec((1, 1, d), lambda i: (i // per_b, 0, 0))],
        out_specs=pl.BlockSpec((tm, d), lambda i: (i, 0)),
        out_shape=jax.ShapeDtypeStruct((t, d), BF16),
        compiler_params=_cparams(("parallel",)),
        name="first_norm",
    )(x2, g.reshape(1, d), scale, shift)


def _inproj_kernel(h_ref, w_ref, ws_ref, o_ref, os_ref):
    h = h_ref[...]
    o_ref[...] = _dot(h, w_ref[...]).astype(o_ref.dtype)

    @pl.when(pl.program_id(1) == 0)
    def _():
        os_ref[...] = _dot(h, ws_ref[...])


def _in_projection(h, w_main, w_small):
    t, d = h.shape
    tm = min(TM_PROJ, t)
    return pl.pallas_call(
        _inproj_kernel,
        grid=(t // tm, N_MAIN // TN_PROJ),
        in_specs=[pl.BlockSpec((tm, d), lambda i, j: (i, 0)),
                  pl.BlockSpec((d, TN_PROJ), lambda i, j: (0, j)),
                  pl.BlockSpec((d, SMALL_W), lambda i, j: (0, 0))],
        out_specs=[pl.BlockSpec((tm, TN_PROJ), lambda i, j: (i, j)),
                   pl.BlockSpec((tm, SMALL_W), lambda i, j: (i, 0))],
        out_shape=[jax.ShapeDtypeStruct((t, N_MAIN), BF16),
                   jax.ShapeDtypeStruct((t, SMALL_W), F32)],
        compiler_params=_cparams(("parallel", "arbitrary")),
        name="in_proj",
    )(h, w_main, w_small)


def _retention_consts():
    h = np.arange(RET_HEADS, dtype=np.float64)
    log_g = np.log(1.0 - 2.0 ** (-5.0 - h))
    idx = np.arange(CHUNK, dtype=np.float64)
    diff = idx[:, None] - idx[None, :]
    intra = np.where(diff >= 0, np.exp(log_g[:, None, None] * np.maximum(diff, 0.0)), 0.0)
    dq = np.exp(log_g[:, None] * (idx + 1.0))
    dk = np.exp(log_g[:, None] * (CHUNK - 1.0 - idx))
    dq = np.broadcast_to(dq[:, :, None], (RET_HEADS, CHUNK, RET_HEAD_DIM))
    dk = np.broadcast_to(dk[:, :, None], (RET_HEADS, CHUNK, RET_HEAD_DIM))
    dchunk = tuple(float(v) for v in np.exp(log_g * CHUNK))
    return (jnp.asarray(intra, F32), jnp.asarray(dq, F32), jnp.asarray(dk, F32), dchunk)


def _ret_kernel(q_ref, k_ref, v_ref, g_ref, cos_ref, sin_ref, di_ref, dq_ref, dk_ref,
                o_ref, s_ref, *, nchunk, dchunk):
    @pl.when(pl.program_id(1) == 0)
    def _():
        s_ref[...] = jnp.zeros_like(s_ref)

    kscale = RET_HEAD_DIM ** -0.5

    def body(c, carry):
        rows = pl.ds(pl.multiple_of(c * CHUNK, CHUNK), CHUNK)
        cs = cos_ref[rows, :]
        sn = sin_ref[rows, :]
        for h in range(RET_HEADS):
            cols = slice(h * RET_HEAD_DIM, (h + 1) * RET_HEAD_DIM)
            q = q_ref[rows, cols].astype(F32)
            k = k_ref[rows, cols].astype(F32)
            v = v_ref[rows, cols]
            qr = q * cs + pltpu.roll(q, RET_HEAD_DIM // 2, 1) * sn
            kr = (k * cs + pltpu.roll(k, RET_HEAD_DIM // 2, 1) * sn) * kscale
            qb = qr.astype(BF16)
            kb = kr.astype(BF16)
            state = s_ref[h]
            s = _dot_nt(qb, kb) * di_ref[h]
            o = _dot(s.astype(BF16), v) + _dot(qb, state.astype(BF16)) * dq_ref[h]
            kd = (kr * dk_ref[h]).astype(BF16)
            s_ref[h] = state * dchunk[h] + _dot_tn(kd, v)
            o = o * lax.rsqrt(jnp.mean(o * o, axis=-1, keepdims=True) + EPS)
            o = o * _silu(g_ref[rows, cols].astype(F32))
            o_ref[rows, cols] = o.astype(o_ref.dtype)
        return carry

    lax.fori_loop(0, nchunk, body, 0)


def _retention(proj, cos_t, sin_t, consts, bsz, seq):
    t = proj.shape[0]
    tb = min(TB_RET, seq)
    per_b = seq // tb
    intra, dq, dk, dchunk = consts
    row = lambda b, j: b * per_b + j
    cblk = lambda idx: pl.BlockSpec((tb, RET_WIDTH), lambda b, j: (row(b, j), idx))
    const3 = pl.BlockSpec((RET_HEADS, CHUNK, RET_HEAD_DIM), lambda b, j: (0, 0, 0))
    tblk = pl.BlockSpec((tb, RET_HEAD_DIM), lambda b, j: (row(b, j), 0))
    base = COL_RET // RET_WIDTH
    return pl.pallas_call(
        functools.partial(_ret_kernel, nchunk=tb // CHUNK, dchunk=dchunk),
        grid=(bsz, per_b),
        in_specs=[cblk(base), cblk(base + 1), cblk(base + 2), cblk(base + 3),
                  tblk, tblk, const3, const3, const3],
        out_specs=pl.BlockSpec((tb, RET_WIDTH), lambda b, j: (row(b, j), 0)),
        out_shape=jax.ShapeDtypeStruct((t, RET_WIDTH), BF16),
        scratch_shapes=[pltpu.VMEM((RET_HEADS, RET_HEAD_DIM, RET_HEAD_DIM), F32)],
        compiler_params=_cparams(("parallel", "arbitrary")),
        name="retention",
    )(proj, proj, proj, proj, cos_t, sin_t, intra, dq, dk)


def _ssd_kernel(xs_ref, z_ref, b_ref, c_ref, sm_ref, cw_ref, cb_ref, bias_ref, alog_ref,
                dskip_ref, ng_ref, tril_ref, y_ref, f_ref,
                ubuf, st_ref, ybuf, fcar, *, tb):
    j = pl.program_id(1)
    tail = CONV_K - 1
    pad = 8

    @pl.when(j == 0)
    def _():
        ubuf[0:pad, :] = jnp.zeros((pad, CONV_DIM), F32)
        st_ref[...] = jnp.zeros_like(st_ref)
        fcar[...] = jnp.zeros_like(fcar)

    @pl.when(j > 0)
    def _():
        ubuf[0:pad, :] = ubuf[tb:tb + pad, :]

    ubuf[pad:pad + tb, 0:SSD_WIDTH] = xs_ref[...].astype(F32)
    ubuf[pad:pad + tb, SSD_WIDTH:SSD_WIDTH + BC_WIDTH] = b_ref[...].astype(F32)
    ubuf[pad:pad + tb, SSD_WIDTH + BC_WIDTH:CONV_DIM] = c_ref[...].astype(F32)

    lane = lax.broadcasted_iota(jnp.int32, (1, LANES), 1)
    is_f = lane < LANE_DT
    half_lane = lane < SSD_HEAD_DIM
    a_row = jnp.where((lane >= LANE_DT) & (lane < LANE_DT + SSD_HEADS), -jnp.exp(alog_ref[...]), 0.0)
    ri = lax.broadcasted_iota(jnp.int32, (CHUNK, CHUNK), 0)
    ci = lax.broadcasted_iota(jnp.int32, (CHUNK, CHUNK), 1)
    causal = ri >= ci
    tril = tril_ref[...]

    for c in range(tb // CHUNK):
        r0 = c * CHUNK
        acc = cb_ref[...] + cw_ref[0:1, :] * ubuf[pad + r0 - tail:pad + r0 - tail + CHUNK, :]
        for kk in range(1, CONV_K):
            acc = acc + cw_ref[kk:kk + 1, :] * ubuf[pad + r0 - tail + kk:pad + r0 - tail + kk + CHUNK, :]
        xc = _silu(acc)
        xs = xc[:, 0:SSD_WIDTH]
        xs_b = xs.astype(BF16)

        pre = sm_ref[r0:r0 + CHUNK, :] + bias_ref[...]
        sp = _softplus(pre)
        val = jnp.where(is_f, -_softplus(-pre), sp * a_row)
        v_hi, v_mid, v_lo = _split3(val)
        cum = _dot(tril, v_hi) + _dot(tril, v_mid) + _dot(tril, v_lo)
        cum = cum + fcar[...]
        fcar[...] = jnp.where(is_f, cum[CHUNK - 1:CHUNK, :], 0.0)
        cum_t = cum.T
        dt_t = sp.T
        n_hi, n_mid, n_lo = _split3(jnp.where(is_f, -cum, 0.0))
        f_ref[r0:r0 + CHUNK, :] = (n_hi.astype(F32) + pltpu.roll(n_mid.astype(F32), FOX_HEADS, 1)
                                   + pltpu.roll(n_lo.astype(F32), 2 * FOX_HEADS, 1)).astype(BF16)
        e_col = jnp.exp(cum)

        for g in range(SSD_GROUPS):
            bm = xc[:, SSD_WIDTH + g * SSD_STATE:SSD_WIDTH + (g + 1) * SSD_STATE]
            cm = xc[:, SSD_WIDTH + BC_WIDTH + g * SSD_STATE:SSD_WIDTH + BC_WIDTH + (g + 1) * SSD_STATE]
            cm_b = cm.astype(BF16)
            cb = _dot_nt(cm_b, bm.astype(BF16))
            bm_t = bm.T
            st_old = st_ref[g]
            inter = _dot(cm_b, st_old.astype(BF16))
            for pr in range(SSD_HEADS_PER_GROUP // 2):
                gp = g * (SSD_HEADS_PER_GROUP // 2) + pr
                xp_b = xs_b[:, gp * LANES:(gp + 1) * LANES]
                ys = []
                ups = []
                ecs = []
                els = []
                for hh in range(2):
                    ln = LANE_DT + 2 * gp + hh
                    a_c = cum[:, ln:ln + 1]
                    a_r = cum_t[ln:ln + 1, :]
                    d_r = dt_t[ln:ln + 1, :]
                    lm = jnp.where(causal, jnp.exp(a_c - a_r), 0.0)
                    m = cb * lm * d_r
                    ys.append(_dot(m.astype(BF16), xp_b))
                    last = a_r[:, CHUNK - 1:CHUNK]
                    w_r = d_r * jnp.exp(last - a_r)
                    ups.append(_dot((bm_t * w_r).astype(BF16), xp_b))
                    ecs.append(e_col[:, ln:ln + 1])
                    els.append(jnp.exp(last))
                lanes = slice(pr * LANES, (pr + 1) * LANES)
                y_pair = (jnp.where(half_lane, ys[0], ys[1])
                          + jnp.where(half_lane, ecs[0], ecs[1]) * inter[:, lanes])
                ybuf[:, gp * LANES:(gp + 1) * LANES] = y_pair
                st_ref[g, :, lanes] = (st_old[:, lanes] * jnp.where(half_lane, els[0], els[1])
                                       + jnp.where(half_lane, ups[0], ups[1]))

        y = ybuf[...] + dskip_ref[...] * xs
        y = y * _silu(z_ref[r0:r0 + CHUNK, :].astype(F32))
        y = y * lax.rsqrt(jnp.mean(y * y, axis=-1, keepdims=True) + EPS) * ng_ref[...]
        y_ref[r0:r0 + CHUNK, :] = y.astype(y_ref.dtype)


def _ssd(proj, small, conv_w, conv_b, bias_row, alog_row, dskip_row, norm_g, tril, bsz, seq):
    t = proj.shape[0]
    tb = min(TB_SSD, seq)
    per_b = seq // tb
    row = lambda b, j: b * per_b + j
    full = lambda shape: pl.BlockSpec(shape, lambda b, j: (0,) * len(shape))
    return pl.pallas_call(
        functools.partial(_ssd_kernel, tb=tb),
        grid=(bsz, per_b),
        in_specs=[pl.BlockSpec((tb, SSD_WIDTH), lambda b, j: (row(b, j), COL_XS // SSD_WIDTH)),
                  pl.BlockSpec((tb, SSD_WIDTH), lambda b, j: (row(b, j), COL_Z // SSD_WIDTH)),
                  pl.BlockSpec((tb, BC_WIDTH), lambda b, j: (row(b, j), COL_B // BC_WIDTH)),
                  pl.BlockSpec((tb, BC_WIDTH), lambda b, j: (row(b, j), COL_C // BC_WIDTH)),
                  pl.BlockSpec((tb, SMALL_W), lambda b, j: (row(b, j), 0)),
                  full((CONV_K, CONV_DIM)), full((1, CONV_DIM)), full((1, SMALL_W)),
                  full((1, SMALL_W)), full((1, SSD_WIDTH)), full((1, SSD_WIDTH)),
                  full((CHUNK, CHUNK))],
        out_specs=[pl.BlockSpec((tb, SSD_WIDTH), lambda b, j: (row(b, j), 0)),
                   pl.BlockSpec((tb, SMALL_W), lambda b, j: (row(b, j), 0))],
        out_shape=[jax.ShapeDtypeStruct((t, SSD_WIDTH), BF16),
                   jax.ShapeDtypeStruct((t, SMALL_W), BF16)],
        scratch_shapes=[pltpu.VMEM((tb + 8, CONV_DIM), F32),
                        pltpu.VMEM((SSD_GROUPS, SSD_STATE, SSD_HEADS_PER_GROUP * SSD_HEAD_DIM), F32),
                        pltpu.VMEM((CHUNK, SSD_WIDTH), F32),
                        pltpu.VMEM((1, SMALL_W), F32)],
        compiler_params=_cparams(("parallel", "arbitrary")),
        name="ssd",
    )(proj, proj, proj, proj, small, conv_w, conv_b, bias_row, alog_row, dskip_row, norm_g, tril)


def _fox_kernel(q_ref, k_ref, v_ref, g_ref, f_ref, o_ref, kaug, vaug, m_ref, acc_ref, *, tq):
    pair = pl.program_id(1)
    i = pl.program_id(2)

    @pl.when(i == 0)
    def _():
        kaug[:, 0:LANES] = k_ref[...]
        kaug[:, LANES:2 * LANES] = f_ref[...]
        vaug[:, 0:LANES] = v_ref[...]
        vaug[:, LANES:2 * LANES] = jnp.ones((vaug.shape[0], LANES), BF16)

    lane = lax.broadcasted_iota(jnp.int32, (1, LANES), 1)
    head0 = lane < FOX_HEAD_DIM
    q = q_ref[...] * (FOX_HEAD_DIM ** -0.5)
    zero = jnp.zeros_like(q)
    qh = []
    for h in range(2):
        hd = 2 * pair + h
        sel = (lane == hd) | (lane == hd + FOX_HEADS) | (lane == hd + 2 * FOX_HEADS)
        ones = jnp.broadcast_to(jnp.where(sel, 1.0, 0.0).astype(BF16), (tq, LANES))
        qm = jnp.where(head0, q, zero) if h == 0 else jnp.where(head0, zero, q)
        qh.append(jnp.concatenate([qm, ones], axis=1))

    m_ref[...] = jnp.full(m_ref.shape, NEG_BIG, F32)
    acc_ref[...] = jnp.zeros_like(acc_ref)

    def block(kb, masked):
        cols = pl.ds(pl.multiple_of(kb * tq, tq), tq)
        k = kaug[cols, :]
        v = vaug[cols, :]
        if masked:
            ri = lax.broadcasted_iota(jnp.int32, (tq, tq), 0)
            ci = lax.broadcasted_iota(jnp.int32, (tq, tq), 1)
            keep = ri >= ci
        for h in range(2):
            s = _dot_nt(qh[h], k)
            if masked:
                s = jnp.where(keep, s, NEG_BIG)
            m_old = m_ref[h]
            m_new = jnp.maximum(m_old, jnp.max(s, axis=-1, keepdims=True))
            alpha = jnp.exp(m_old - m_new)
            p = jnp.exp(s - jnp.concatenate([m_new] * (tq // LANES), axis=1))
            acc_ref[h] = (jnp.concatenate([alpha, alpha], axis=1) * acc_ref[h]
                          + _dot(p.astype(BF16), v))
            m_ref[h] = m_new

    def body(kb, carry):
        block(kb, False)
        return carry

    lax.fori_loop(0, i, body, 0)
    block(i, True)

    o = jnp.where(head0, acc_ref[0, :, 0:LANES] / acc_ref[0, :, LANES:2 * LANES],
                  acc_ref[1, :, 0:LANES] / acc_ref[1, :, LANES:2 * LANES])
    o = o * _silu(g_ref[...].astype(F32))
    o_ref[...] = o.astype(o_ref.dtype)


def _fox(proj, f_aug, bsz, seq):
    t = proj.shape[0]
    tq = min(T_ATT, seq)
    nq = seq // tq
    base = COL_FOX // LANES
    nblk = FOX_WIDTH // LANES
    return pl.pallas_call(
        functools.partial(_fox_kernel, tq=tq),
        grid=(bsz, FOX_PAIRS, nq),
        in_specs=[pl.BlockSpec((tq, LANES), lambda b, p, i: (b * nq + i, base + p)),
                  pl.BlockSpec((seq, LANES), lambda b, p, i: (b, base + nblk + p)),
                  pl.BlockSpec((seq, LANES), lambda b, p, i: (b, base + 2 * nblk + p)),
                  pl.BlockSpec((tq, LANES), lambda b, p, i: (b * nq + i, base + 3 * nblk + p)),
                  pl.BlockSpec((seq, SMALL_W), lambda b, p, i: (b, 0))],
        out_specs=pl.BlockSpec((tq, LANES), lambda b, p, i: (b * nq + i, p)),
        out_shape=jax.ShapeDtypeStruct((t, FOX_WIDTH), BF16),
        scratch_shapes=[pltpu.VMEM((seq, 2 * LANES), BF16),
                        pltpu.VMEM((seq, 2 * LANES), BF16),
                        pltpu.VMEM((2, tq, LANES), F32),
                        pltpu.VMEM((2, tq, 2 * LANES), F32)],
        compiler_params=_cparams(("parallel", "parallel", "arbitrary")),
        name="fox_attention",
    )(proj, proj, proj, proj, f_aug)


def _out_kernel(ret_ref, ssd_ref, fox_ref, w_ref, x_ref, gate_ref, g_ref, sc_ref, sh_ref,
                *out_refs, emit_x):
    acc = _dot(ret_ref[...], w_ref[0:RET_WIDTH, :])
    acc = acc + _dot(ssd_ref[...], w_ref[RET_WIDTH:RET_WIDTH + SSD_WIDTH, :])
    acc = acc + _dot(fox_ref[...], w_ref[RET_WIDTH + SSD_WIDTH:MIX_WIDTH, :])
    xn = x_ref[...] + gate_ref[0] * acc
    if emit_x:
        out_refs[0][...] = xn
    y_ref = out_refs[-1]
    y_ref[...] = _modnorm(xn, g_ref[...], sc_ref[0], sh_ref[0]).astype(y_ref.dtype)


def _out_projection(ret, ssd, fox, w_out, x2, gate, g, scale, shift, seq, y_dtype, emit_x):
    t, d = x2.shape
    tm = min(TM_OUT, seq)
    per_b = seq // tm
    rowblk = lambda w: pl.BlockSpec((tm, w), lambda i: (i, 0))
    perb = pl.BlockSpec((1, 1, d), lambda i: (i // per_b, 0, 0))
    out_specs = [rowblk(d)]
    out_shape = [jax.ShapeDtypeStruct((t, d), y_dtype)]
    if emit_x:
        out_specs = [rowblk(d)] + out_specs
        out_shape = [jax.ShapeDtypeStruct((t, d), F32)] + out_shape
    return pl.pallas_call(
        functools.partial(_out_kernel, emit_x=emit_x),
        grid=(t // tm,),
        in_specs=[rowblk(RET_WIDTH), rowblk(SSD_WIDTH), rowblk(FOX_WIDTH),
                  pl.BlockSpec((MIX_WIDTH, d), lambda i: (0, 0)),
                  rowblk(d), perb, pl.BlockSpec((1, d), lambda i: (0, 0)), perb, perb],
        out_specs=out_specs,
        out_shape=out_shape,
        compiler_params=_cparams(("parallel",)),
        name="out_proj",
    )(ret, ssd, fox, w_out, x2, gate, g.reshape(1, d), scale, shift)


def _permute_in_weights(w_in):
    o_xbc = 4 * RET_WIDTH
    o_dt = o_xbc + CONV_DIM
    o_z = o_dt + SSD_HEADS
    o_fox = o_z + SSD_WIDTH
    o_f = o_fox + 4 * FOX_WIDTH
    main = jnp.concatenate([
        w_in[..., 0:o_xbc],
        w_in[..., o_xbc:o_xbc + SSD_WIDTH],
        w_in[..., o_z:o_z + SSD_WIDTH],
        w_in[..., o_xbc + SSD_WIDTH:o_dt],
        w_in[..., o_fox:o_f],
    ], axis=-1).astype(BF16)
    pad = jnp.zeros(w_in.shape[:-1] + (SMALL_W - FOX_HEADS - SSD_HEADS,), w_in.dtype)
    small = jnp.concatenate([w_in[..., o_f:o_f + FOX_HEADS], w_in[..., o_dt:o_z], pad],
                            axis=-1).astype(BF16)
    return main, small


def _lane_row(depth, pieces):
    row = jnp.zeros((depth, 1, SMALL_W), F32)
    for start, arr in pieces:
        row = row.at[:, 0, start:start + arr.shape[-1]].set(arr.astype(F32))
    return row


def kernel(x, c, positions, norm_g, w_ada, b_ada, w_in, conv_w, conv_b, dt_bias, a_log, d_skip,
           ssd_norm_g, b_forget, w_out, final_g):
    bsz, seq, d = x.shape
    depth = w_in.shape[0]
    t = bsz * seq

    mod = _modulation(c, w_ada, b_ada)
    shift = mod[:, :, 0:d].reshape(depth, bsz, 1, d)
    scale = mod[:, :, d:2 * d].reshape(depth, bsz, 1, d)
    gate = mod[:, :, 2 * d:3 * d].reshape(depth, bsz, 1, d)

    cos_t, sin_t = _rope_tables(positions)
    w_main, w_small = _permute_in_weights(w_in)
    w_out_b = w_out.astype(BF16)
    ret_consts = _retention_consts()
    tril = jnp.asarray(np.tril(np.ones((CHUNK, CHUNK), np.float32)), BF16)
    bias_rows = _lane_row(depth, [(LANE_F, b_forget), (LANE_DT, dt_bias)])
    alog_rows = _lane_row(depth, [(LANE_DT, a_log)])
    dskip_rows = jnp.repeat(d_skip, SSD_HEAD_DIM, axis=-1).reshape(depth, 1, SSD_WIDTH)
    zeros_bd = jnp.zeros((bsz, 1, d), F32)

    x2 = x.reshape(t, d)
    h = _first_norm(x2, norm_g[0], scale[0], shift[0], seq)
    out = None
    for l in range(depth):
        proj, small = _in_projection(h, w_main[l], w_small[l])
        ret = _retention(proj, cos_t, sin_t, ret_consts, bsz, seq)
        ssd, f_rows = _ssd(proj, small, conv_w[l], conv_b[l].reshape(1, CONV_DIM), bias_rows[l],
                           alog_rows[l], dskip_rows[l], ssd_norm_g[l].reshape(1, SSD_WIDTH), tril,
                           bsz, seq)
        fox = _fox(proj, f_rows, bsz, seq)
        if l + 1 < depth:
            x2, h = _out_projection(ret, ssd, fox, w_out_b[l], x2, gate[l], norm_g[l + 1],
                                    scale[l + 1], shift[l + 1], seq, BF16, True)
        else:
            (out,) = _out_projection(ret, ssd, fox, w_out_b[l], x2, gate[l], final_g,
                                     zeros_bd, zeros_bd, seq, F32, False)
    return out.reshape(bsz, seq, d)
```

```python
import functools
import math

import numpy as np
import jax
import jax.numpy as jnp
from jax import lax
from jax.experimental import pallas as pl
from jax.experimental.pallas import tpu as pltpu

F32 = jnp.float32
BF16 = jnp.bfloat16

RET_HEADS = 4
RET_HEAD_DIM = 128
RET_WIDTH = RET_HEADS * RET_HEAD_DIM
SSD_HEADS = 16
SSD_HEAD_DIM = 64
SSD_WIDTH = SSD_HEADS * SSD_HEAD_DIM
SSD_GROUPS = 2
SSD_STATE = 128
SSD_HEADS_PER_GROUP = SSD_HEADS // SSD_GROUPS
CONV_K = 4
BC_WIDTH = SSD_GROUPS * SSD_STATE
CONV_DIM = SSD_WIDTH + 2 * BC_WIDTH
FOX_HEADS = 8
FOX_HEAD_DIM = 64
FOX_WIDTH = FOX_HEADS * FOX_HEAD_DIM
FOX_PAIRS = FOX_HEADS // 2
MIX_WIDTH = RET_WIDTH + SSD_WIDTH + FOX_WIDTH
CHUNK = 128
ROPE_BASE = 10000.0
EPS = 1e-6
LOG2E = math.log2(math.e)
FOX_Q_SCALE = FOX_HEAD_DIM ** -0.5 * LOG2E

COL_RET = 0
COL_XS = 4 * RET_WIDTH
COL_Z = COL_XS + SSD_WIDTH
COL_B = COL_Z + SSD_WIDTH
COL_C = COL_B + BC_WIDTH
COL_FOX = COL_C + BC_WIDTH
N_MAIN = COL_FOX + 4 * FOX_WIDTH
SMALL_W = 128
LANE_F = 0
LANE_DT = FOX_HEADS

LANES = 128
TM_PROJ = 1024
TN_PROJ = 1664
TM_OUT = 512
TM_NORM = 1024
TB_RET = 1024
TB_SSD = 512
T_ATT = 512
KV_UNROLL = 4
VMEM_LIMIT = 48 * 1024 * 1024
NEG_BIG = -1e30


def _cparams(sem):
    return pltpu.CompilerParams(dimension_semantics=sem, vmem_limit_bytes=VMEM_LIMIT)


def _dot(a, b):
    return jnp.dot(a, b, preferred_element_type=F32)


def _dot_nt(a, b):
    return lax.dot_general(a, b, (((1,), (1,)), ((), ())), preferred_element_type=F32)


def _dot_tn(a, b):
    return lax.dot_general(a, b, (((0,), (0,)), ((), ())), preferred_element_type=F32)


def _silu(t):
    return t * jax.nn.sigmoid(t)


def _split3(t):
    hi = t.astype(BF16)
    r1 = t - hi.astype(F32)
    mid = r1.astype(BF16)
    lo = (r1 - mid.astype(F32)).astype(BF16)
    return hi, mid, lo


def _modnorm(t, g, scale, shift):
    var = jnp.mean(t * t, axis=-1, keepdims=True)
    return t * lax.rsqrt(var + EPS) * g * (1.0 + scale) + shift


def _mod_kernel(c_ref, w_ref, b_ref, o_ref):
    a = _silu(c_ref[...])
    w = w_ref[0]
    a_hi = a.astype(BF16)
    a_lo = (a - a_hi.astype(F32)).astype(BF16)
    w_hi = w.astype(BF16)
    w_lo = (w - w_hi.astype(F32)).astype(BF16)
    o_ref[0] = _dot(a_hi, w_hi) + _dot(a_hi, w_lo) + _dot(a_lo, w_hi) + b_ref[0]


def _modulation(c, w_ada, b_ada):
    depth, d, n3 = w_ada.shape
    bsz = c.shape[0]
    rows = 16
    c_pad = jnp.zeros((rows, d), F32).at[:bsz].set(c)
    tn = 1024
    out = pl.pallas_call(
        _mod_kernel,
        grid=(depth, n3 // tn),
        in_specs=[pl.BlockSpec((rows, d), lambda l, j: (0, 0)),
                  pl.BlockSpec((1, d, tn), lambda l, j: (l, 0, j)),
                  pl.BlockSpec((1, 1, tn), lambda l, j: (l, 0, j))],
        out_specs=pl.BlockSpec((1, rows, tn), lambda l, j: (l, 0, j)),
        out_shape=jax.ShapeDtypeStruct((depth, rows, n3), F32),
        compiler_params=_cparams(("parallel", "parallel")),
        name="adaln_mod",
    )(c_pad, w_ada, b_ada.reshape(depth, 1, n3))
    return out[:, :bsz]


def _rope_kernel(pos_ref, freq_ref, cos_ref, sin_ref):
    pos = pos_ref[0].astype(F32)
    ang = freq_ref[...] * pos
    c = jnp.cos(ang)
    s = jnp.sin(ang)
    cos_ref[...] = jnp.concatenate([c, c], axis=0).T
    sin_ref[...] = jnp.concatenate([-s, s], axis=0).T


def _rope_tables(positions):
    bsz, seq = positions.shape
    half = RET_HEAD_DIM // 2
    freq = (ROPE_BASE ** (-jnp.arange(half, dtype=F32) / half)).reshape(half, 1)
    tm = 512
    nt = seq // tm
    shp = jax.ShapeDtypeStruct((bsz * seq, RET_HEAD_DIM), F32)
    return pl.pallas_call(
        _rope_kernel,
        grid=(bsz, nt),
        in_specs=[pl.BlockSpec((1, 1, tm), lambda b, i: (b, 0, i)),
                  pl.BlockSpec((half, 1), lambda b, i: (0, 0))],
        out_specs=[pl.BlockSpec((tm, RET_HEAD_DIM), lambda b, i: (b * nt + i, 0)),
                   pl.BlockSpec((tm, RET_HEAD_DIM), lambda b, i: (b * nt + i, 0))],
        out_shape=[shp, shp],
        compiler_params=_cparams(("parallel", "parallel")),
        name="rope_tables",
    )(positions.reshape(bsz, 1, seq), freq)


def _norm_kernel(x_ref, g_ref, sc_ref, sh_ref, h_ref):
    h_ref[...] = _modnorm(x_ref[...], g_ref[...], sc_ref[0], sh_ref[0]).astype(h_ref.dtype)


def _first_norm(x2, g, scale, shift, seq):
    t, d = x2.shape
    tm = min(TM_NORM, seq)
    per_b = seq // tm
    return pl.pallas_call(
        _norm_kernel,
        grid=(t // tm,),
        in_specs=[pl.BlockSpec((tm, d), lambda i: (i, 0)),
                  pl.BlockSpec((1, d), lambda i: (0, 0)),
                  pl.BlockSpec((1, 1, d), lambda i: (i // per_b, 0, 0)),
                  pl.BlockSpec((1, 1, d), lambda i: (i // per_b, 0, 0))],
        out_specs=pl.BlockSpec((tm, d), lambda i: (i, 0)),
        out_shape=jax.ShapeDtypeStruct((t, d), BF16),
        compiler_params=_cparams(("parallel",)),
        name="first_norm",
    )(x2, g.reshape(1, d), scale, shift)


def _inproj_kernel(h_ref, w_ref, ws_ref, o_ref, os_ref):
    h = h_ref[...]
    o_ref[...] = _dot(h, w_ref[...]).astype(o_ref.dtype)

    @pl.when(pl.program_id(1) == 0)
    def _():
        os_ref[...] = _dot(h, ws_ref[...])


def _in_projection(h, w_main, w_small):
    t, d = h.shape
    tm = min(TM_PROJ, t)
    return pl.pallas_call(
        _inproj_kernel,
        grid=(t // tm, N_MAIN // TN_PROJ),
        in_specs=[pl.BlockSpec((tm, d), lambda i, j: (i, 0)),
                  pl.BlockSpec((d, TN_PROJ), lambda i, j: (0, j)),
                  pl.BlockSpec((d, SMALL_W), lambda i, j: (0, 0))],
        out_specs=[pl.BlockSpec((tm, TN_PROJ), lambda i, j: (i, j)),
                   pl.BlockSpec((tm, SMALL_W), lambda i, j: (i, 0))],
        out_shape=[jax.ShapeDtypeStruct((t, N_MAIN), BF16),
                   jax.ShapeDtypeStruct((t, SMALL_W), F32)],
        compiler_params=_cparams(("parallel", "arbitrary")),
        name="in_proj",
    )(h, w_main, w_small)


def _retention_consts():
    h = np.arange(RET_HEADS, dtype=np.float64)
    log_g = np.log(1.0 - 2.0 ** (-5.0 - h))
    idx = np.arange(CHUNK, dtype=np.float64)
    diff = idx[:, None] - idx[None, :]
    intra = np.where(diff >= 0, np.exp(log_g[:, None, None] * np.maximum(diff, 0.0)), 0.0)
    dq = np.exp(log_g[:, None] * (idx + 1.0))
    dk = np.exp(log_g[:, None] * (CHUNK - 1.0 - idx))
    dq = np.broadcast_to(dq[:, :, None], (RET_HEADS, CHUNK, RET_HEAD_DIM))
    dk = np.broadcast_to(dk[:, :, None], (RET_HEADS, CHUNK, RET_HEAD_DIM))
    dchunk = tuple(float(v) for v in np.exp(log_g * CHUNK))
    return (jnp.asarray(intra, F32), jnp.asarray(dq, F32), jnp.asarray(dk, F32), dchunk)


def _ret_kernel(q_ref, k_ref, v_ref, g_ref, cos_ref, sin_ref, di_ref, dq_ref, dk_ref,
                o_ref, s_ref, *, nchunk, dchunk):
    @pl.when(pl.program_id(1) == 0)
    def _():
        s_ref[...] = jnp.zeros_like(s_ref)

    kscale = RET_HEAD_DIM ** -0.5

    def body(c, carry):
        rows = pl.ds(pl.multiple_of(c * CHUNK, CHUNK), CHUNK)
        cs = cos_ref[rows, :]
        sn = sin_ref[rows, :]
        for h in range(RET_HEADS):
            cols = slice(h * RET_HEAD_DIM, (h + 1) * RET_HEAD_DIM)
            q = q_ref[rows, cols].astype(F32)
            k = k_ref[rows, cols].astype(F32)
            v = v_ref[rows, cols]
            qr = q * cs + pltpu.roll(q, RET_HEAD_DIM // 2, 1) * sn
            kr = (k * cs + pltpu.roll(k, RET_HEAD_DIM // 2, 1) * sn) * kscale
            qb = qr.astype(BF16)
            kb = kr.astype(BF16)
            state = s_ref[h]
            s = _dot_nt(qb, kb) * di_ref[h]
            o = _dot(s.astype(BF16), v) + _dot(qb, state.astype(BF16)) * dq_ref[h]
            kd = (kr * dk_ref[h]).astype(BF16)
            s_ref[h] = state * dchunk[h] + _dot_tn(kd, v)
            o = o * lax.rsqrt(jnp.mean(o * o, axis=-1, keepdims=True) + EPS)
            o = o * _silu(g_ref[rows, cols].astype(F32))
            o_ref[rows, cols] = o.astype(o_ref.dtype)
        return carry

    lax.fori_loop(0, nchunk, body, 0, unroll=2)


def _retention(proj, cos_t, sin_t, consts, bsz, seq):
    t = proj.shape[0]
    tb = min(TB_RET, seq)
    per_b = seq // tb
    intra, dq, dk, dchunk = consts
    row = lambda b, j: b * per_b + j
    cblk = lambda idx: pl.BlockSpec((tb, RET_WIDTH), lambda b, j: (row(b, j), idx))
    const3 = pl.BlockSpec((RET_HEADS, CHUNK, RET_HEAD_DIM), lambda b, j: (0, 0, 0))
    tblk = pl.BlockSpec((tb, RET_HEAD_DIM), lambda b, j: (row(b, j), 0))
    base = COL_RET // RET_WIDTH
    return pl.pallas_call(
        functools.partial(_ret_kernel, nchunk=tb // CHUNK, dchunk=dchunk),
        grid=(bsz, per_b),
        in_specs=[cblk(base), cblk(base + 1), cblk(base + 2), cblk(base + 3),
                  tblk, tblk, const3, const3, const3],
        out_specs=pl.BlockSpec((tb, RET_WIDTH), lambda b, j: (row(b, j), 0)),
        out_shape=jax.ShapeDtypeStruct((t, RET_WIDTH), BF16),
        scratch_shapes=[pltpu.VMEM((RET_HEADS, RET_HEAD_DIM, RET_HEAD_DIM), F32)],
        compiler_params=_cparams(("parallel", "arbitrary")),
        name="retention",
    )(proj, proj, proj, proj, cos_t, sin_t, intra, dq, dk)


def _ssd_kernel(xs_ref, z_ref, b_ref, c_ref, sm_ref, cw_ref, cb_ref, bias_ref, alog_ref,
                dskip_ref, ng_ref, tril_ref, y_ref, f_ref,
                ubuf, st_ref, ybuf, fcar, *, tb):
    j = pl.program_id(1)
    tail = CONV_K - 1
    pad = 8

    @pl.when(j == 0)
    def _():
        ubuf[0:pad, :] = jnp.zeros((pad, CONV_DIM), F32)
        st_ref[...] = jnp.zeros_like(st_ref)
        fcar[...] = jnp.zeros_like(fcar)

    @pl.when(j > 0)
    def _():
        ubuf[0:pad, :] = ubuf[tb:tb + pad, :]

    ubuf[pad:pad + tb, 0:SSD_WIDTH] = xs_ref[...].astype(F32)
    ubuf[pad:pad + tb, SSD_WIDTH:SSD_WIDTH + BC_WIDTH] = b_ref[...].astype(F32)
    ubuf[pad:pad + tb, SSD_WIDTH + BC_WIDTH:CONV_DIM] = c_ref[...].astype(F32)

    lane = lax.broadcasted_iota(jnp.int32, (1, LANES), 1)
    is_f = lane < LANE_DT
    half_lane = lane < SSD_HEAD_DIM
    a_row = jnp.where((lane >= LANE_DT) & (lane < LANE_DT + SSD_HEADS), -jnp.exp(alog_ref[...]), 0.0)
    ri = lax.broadcasted_iota(jnp.int32, (CHUNK, CHUNK), 0)
    ci = lax.broadcasted_iota(jnp.int32, (CHUNK, CHUNK), 1)
    causal = ri >= ci
    tril = tril_ref[...]

    for c in range(tb // CHUNK):
        r0 = c * CHUNK
        acc = cb_ref[...] + cw_ref[0:1, :] * ubuf[pad + r0 - tail:pad + r0 - tail + CHUNK, :]
        for kk in range(1, CONV_K):
            acc = acc + cw_ref[kk:kk + 1, :] * ubuf[pad + r0 - tail + kk:pad + r0 - tail + kk + CHUNK, :]
        xc = _silu(acc)
        xs = xc[:, 0:SSD_WIDTH]
        xs_b = xs.astype(BF16)

        pre = sm_ref[r0:r0 + CHUNK, :] + bias_ref[...]
        tail_term = jnp.log1p(jnp.exp(-jnp.abs(pre)))
        sp = jnp.maximum(pre, 0.0) + tail_term
        val = jnp.where(is_f, -(jnp.maximum(-pre, 0.0) + tail_term), sp * a_row)
        v_hi, v_mid, v_lo = _split3(val)
        cum = _dot(tril, v_hi) + _dot(tril, v_mid) + _dot(tril, v_lo)
        cum = cum + fcar[...]
        fcar[...] = jnp.where(is_f, cum[CHUNK - 1:CHUNK, :], 0.0)
        cum_t = cum.T
        dt_t = sp.T
        n_hi, n_mid, n_lo = _split3(jnp.where(is_f, cum * -LOG2E, 0.0))
        f_ref[r0:r0 + CHUNK, :] = (n_hi.astype(F32) + pltpu.roll(n_mid.astype(F32), FOX_HEADS, 1)
                                   + pltpu.roll(n_lo.astype(F32), 2 * FOX_HEADS, 1)).astype(BF16)
        e_col = jnp.exp(cum)

        for g in range(SSD_GROUPS):
            bm = xc[:, SSD_WIDTH + g * SSD_STATE:SSD_WIDTH + (g + 1) * SSD_STATE]
            cm = xc[:, SSD_WIDTH + BC_WIDTH + g * SSD_STATE:SSD_WIDTH + BC_WIDTH + (g + 1) * SSD_STATE]
            cm_b = cm.astype(BF16)
            cb = _dot_nt(cm_b, bm.astype(BF16))
            bm_t = bm.T
            st_old = st_ref[g]
            inter = _dot(cm_b, st_old.astype(BF16))
            for pr in range(SSD_HEADS_PER_GROUP // 2):
                gp = g * (SSD_HEADS_PER_GROUP // 2) + pr
                xp_b = xs_b[:, gp * LANES:(gp + 1) * LANES]
                ys = []
                ups = []
                ecs = []
                els = []
                for hh in range(2):
                    ln = LANE_DT + 2 * gp + hh
                    a_c = cum[:, ln:ln + 1]
                    a_r = cum_t[ln:ln + 1, :]
                    d_r = dt_t[ln:ln + 1, :]
                    lm = jnp.where(causal, jnp.exp(a_c - a_r), 0.0)
                    m = cb * lm * d_r
                    ys.append(_dot(m.astype(BF16), xp_b))
                    last = a_r[:, CHUNK - 1:CHUNK]
                    w_r = d_r * jnp.exp(last - a_r)
                    ups.append(_dot((bm_t * w_r).astype(BF16), xp_b))
                    ecs.append(e_col[:, ln:ln + 1])
                    els.append(jnp.exp(last))
                lanes = slice(pr * LANES, (pr + 1) * LANES)
                y_pair = (jnp.where(half_lane, ys[0], ys[1])
                          + jnp.where(half_lane, ecs[0], ecs[1]) * inter[:, lanes])
                ybuf[:, gp * LANES:(gp + 1) * LANES] = y_pair
                st_ref[g, :, lanes] = (st_old[:, lanes] * jnp.where(half_lane, els[0], els[1])
                                       + jnp.where(half_lane, ups[0], ups[1]))

        y = ybuf[...] + dskip_ref[...] * xs
        y = y * _silu(z_ref[r0:r0 + CHUNK, :].astype(F32))
        y = y * lax.rsqrt(jnp.mean(y * y, axis=-1, keepdims=True) + EPS) * ng_ref[...]
        y_ref[r0:r0 + CHUNK, :] = y.astype(y_ref.dtype)


def _ssd(proj, small, conv_w, conv_b, bias_row, alog_row, dskip_row, norm_g, tril, bsz, seq):
    t = proj.shape[0]
    tb = min(TB_SSD, seq)
    per_b = seq // tb
    row = lambda b, j: b * per_b + j
    full = lambda shape: pl.BlockSpec(shape, lambda b, j: (0,) * len(shape))
    return pl.pallas_call(
        functools.partial(_ssd_kernel, tb=tb),
        grid=(bsz, per_b),
        in_specs=[pl.BlockSpec((tb, SSD_WIDTH), lambda b, j: (row(b, j), COL_XS // SSD_WIDTH)),
                  pl.BlockSpec((tb, SSD_WIDTH), lambda b, j: (row(b, j), COL_Z // SSD_WIDTH)),
                  pl.BlockSpec((tb, BC_WIDTH), lambda b, j: (row(b, j), COL_B // BC_WIDTH)),
                  pl.BlockSpec((tb, BC_WIDTH), lambda b, j: (row(b, j), COL_C // BC_WIDTH)),
                  pl.BlockSpec((tb, SMALL_W), lambda b, j: (row(b, j), 0)),
                  full((CONV_K, CONV_DIM)), full((1, CONV_DIM)), full((1, SMALL_W)),
                  full((1, SMALL_W)), full((1, SSD_WIDTH)), full((1, SSD_WIDTH)),
                  full((CHUNK, CHUNK))],
        out_specs=[pl.BlockSpec((tb, SSD_WIDTH), lambda b, j: (row(b, j), 0)),
                   pl.BlockSpec((tb, SMALL_W), lambda b, j: (row(b, j), 0))],
        out_shape=[jax.ShapeDtypeStruct((t, SSD_WIDTH), BF16),
                   jax.ShapeDtypeStruct((t, SMALL_W), BF16)],
        scratch_shapes=[pltpu.VMEM((tb + 8, CONV_DIM), F32),
                        pltpu.VMEM((SSD_GROUPS, SSD_STATE, SSD_HEADS_PER_GROUP * SSD_HEAD_DIM), F32),
                        pltpu.VMEM((CHUNK, SSD_WIDTH), F32),
                        pltpu.VMEM((1, SMALL_W), F32)],
        compiler_params=_cparams(("parallel", "arbitrary")),
        name="ssd",
    )(proj, proj, proj, proj, small, conv_w, conv_b, bias_row, alog_row, dskip_row, norm_g, tril)


def _fox_kernel(q_ref, k_ref, v_ref, g_ref, f_ref, o_ref, kaug, vaug, m_ref, acc_ref, *, tq):
    pair = pl.program_id(1)
    i = pl.program_id(2)

    @pl.when(i == 0)
    def _():
        kaug[:, 0:LANES] = k_ref[...]
        kaug[:, LANES:2 * LANES] = f_ref[...]
        vaug[:, 0:LANES] = v_ref[...]
        vaug[:, LANES:2 * LANES] = jnp.ones((vaug.shape[0], LANES), BF16)

    lane = lax.broadcasted_iota(jnp.int32, (1, LANES), 1)
    head0 = lane < FOX_HEAD_DIM
    q = q_ref[...]
    zero = jnp.zeros_like(q)
    qh = []
    for h in range(2):
        hd = 2 * pair + h
        sel = (lane == hd) | (lane == hd + FOX_HEADS) | (lane == hd + 2 * FOX_HEADS)
        ones = jnp.broadcast_to(jnp.where(sel, 1.0, 0.0).astype(BF16), (tq, LANES))
        qm = jnp.where(head0, q, zero) if h == 0 else jnp.where(head0, zero, q)
        qh.append(jnp.concatenate([qm, ones], axis=1))

    m_ref[...] = jnp.full(m_ref.shape, NEG_BIG, F32)
    acc_ref[...] = jnp.zeros_like(acc_ref)

    def block(kb, masked):
        cols = pl.ds(pl.multiple_of(kb * tq, tq), tq)
        k = kaug[cols, :]
        v = vaug[cols, :]
        if masked:
            ri = lax.broadcasted_iota(jnp.int32, (tq, tq), 0)
            ci = lax.broadcasted_iota(jnp.int32, (tq, tq), 1)
            keep = ri >= ci
        for h in range(2):
            s = _dot_nt(qh[h], k)
            if masked:
                s = jnp.where(keep, s, NEG_BIG)
            m_old = m_ref[h]
            m_new = jnp.maximum(m_old, jnp.max(s, axis=-1, keepdims=True))
            alpha = jnp.exp2(m_old - m_new)
            p = jnp.exp2(s - jnp.concatenate([m_new] * (tq // LANES), axis=1))
            acc_ref[h] = (jnp.concatenate([alpha, alpha], axis=1) * acc_ref[h]
                          + _dot(p.astype(BF16), v))
            m_ref[h] = m_new

    def body(kq, carry):
        for u in range(KV_UNROLL):
            block(KV_UNROLL * kq + u, False)
        return carry

    full = i // KV_UNROLL
    lax.fori_loop(0, full, body, 0)
    for extra in range(KV_UNROLL):
        @pl.when(i - full * KV_UNROLL == extra)
        def _(extra=extra):
            for u in range(extra):
                block(full * KV_UNROLL + u, False)
            block(i, True)

    o = jnp.where(head0, acc_ref[0, :, 0:LANES] / acc_ref[0, :, LANES:2 * LANES],
                  acc_ref[1, :, 0:LANES] / acc_ref[1, :, LANES:2 * LANES])
    o = o * _silu(g_ref[...].astype(F32))
    o_ref[...] = o.astype(o_ref.dtype)


def _fox(proj, f_aug, bsz, seq):
    t = proj.shape[0]
    tq = min(T_ATT, seq)
    nq = seq // tq
    base = COL_FOX // LANES
    nblk = FOX_WIDTH // LANES
    return pl.pallas_call(
        functools.partial(_fox_kernel, tq=tq),
        grid=(bsz, FOX_PAIRS, nq),
        in_specs=[pl.BlockSpec((tq, LANES), lambda b, p, i: (b * nq + i, base + p)),
                  pl.BlockSpec((seq, LANES), lambda b, p, i: (b, base + nblk + p)),
                  pl.BlockSpec((seq, LANES), lambda b, p, i: (b, base + 2 * nblk + p)),
                  pl.BlockSpec((tq, LANES), lambda b, p, i: (b * nq + i, base + 3 * nblk + p)),
                  pl.BlockSpec((seq, SMALL_W), lambda b, p, i: (b, 0))],
        out_specs=pl.BlockSpec((tq, LANES), lambda b, p, i: (b * nq + i, p)),
        out_shape=jax.ShapeDtypeStruct((t, FOX_WIDTH), BF16),
        scratch_shapes=[pltpu.VMEM((seq, 2 * LANES), BF16),
                        pltpu.VMEM((seq, 2 * LANES), BF16),
                        pltpu.VMEM((2, tq, LANES), F32),
                        pltpu.VMEM((2, tq, 2 * LANES), F32)],
        compiler_params=_cparams(("parallel", "parallel", "arbitrary")),
        name="fox_attention",
    )(proj, proj, proj, proj, f_aug)


def _out_kernel(ret_ref, ssd_ref, fox_ref, w_ref, x_ref, gate_ref, g_ref, sc_ref, sh_ref,
                *out_refs, emit_x):
    acc = _dot(ret_ref[...], w_ref[0:RET_WIDTH, :])
    acc = acc + _dot(ssd_ref[...], w_ref[RET_WIDTH:RET_WIDTH + SSD_WIDTH, :])
    acc = acc + _dot(fox_ref[...], w_ref[RET_WIDTH + SSD_WIDTH:MIX_WIDTH, :])
    xn = x_ref[...] + gate_ref[0] * acc
    if emit_x:
        out_refs[0][...] = xn
    y_ref = out_refs[-1]
    y_ref[...] = _modnorm(xn, g_ref[...], sc_ref[0], sh_ref[0]).astype(y_ref.dtype)


def _out_projection(ret, ssd, fox, w_out, x2, gate, g, scale, shift, seq, y_dtype, emit_x):
    t, d = x2.shape
    tm = min(TM_OUT, seq)
    per_b = seq // tm
    rowblk = lambda w: pl.BlockSpec((tm, w), lambda i: (i, 0))
    perb = pl.BlockSpec((1, 1, d), lambda i: (i // per_b, 0, 0))
    out_specs = [rowblk(d)]
    out_shape = [jax.ShapeDtypeStruct((t, d), y_dtype)]
    if emit_x:
        out_specs = [rowblk(d)] + out_specs
        out_shape = [jax.ShapeDtypeStruct((t, d), F32)] + out_shape
    return pl.pallas_call(
        functools.partial(_out_kernel, emit_x=emit_x),
        grid=(t // tm,),
        in_specs=[rowblk(RET_WIDTH), rowblk(SSD_WIDTH), rowblk(FOX_WIDTH),
                  pl.BlockSpec((MIX_WIDTH, d), lambda i: (0, 0)),
                  rowblk(d), perb, pl.BlockSpec((1, d), lambda i: (0, 0)), perb, perb],
        out_specs=out_specs,
        out_shape=out_shape,
        compiler_params=_cparams(("parallel",)),
        name="out_proj",
    )(ret, ssd, fox, w_out, x2, gate, g.reshape(1, d), scale, shift)


def _permute_in_weights(w_in):
    o_xbc = 4 * RET_WIDTH
    o_dt = o_xbc + CONV_DIM
    o_z = o_dt + SSD_HEADS
    o_fox = o_z + SSD_WIDTH
    o_f = o_fox + 4 * FOX_WIDTH
    main = jnp.concatenate([
        w_in[..., 0:o_xbc],
        w_in[..., o_xbc:o_xbc + SSD_WIDTH],
        w_in[..., o_z:o_z + SSD_WIDTH],
        w_in[..., o_xbc + SSD_WIDTH:o_dt],
        w_in[..., o_fox:o_fox + FOX_WIDTH] * FOX_Q_SCALE,
        w_in[..., o_fox + FOX_WIDTH:o_f],
    ], axis=-1).astype(BF16)
    pad = jnp.zeros(w_in.shape[:-1] + (SMALL_W - FOX_HEADS - SSD_HEADS,), w_in.dtype)
    small = jnp.concatenate([w_in[..., o_f:o_f + FOX_HEADS], w_in[..., o_dt:o_z], pad],
                            axis=-1).astype(BF16)
    return main, small


def _lane_row(depth, pieces):
    row = jnp.zeros((depth, 1, SMALL_W), F32)
    for start, arr in pieces:
        row = row.at[:, 0, start:start + arr.shape[-1]].set(arr.astype(F32))
    return row


def kernel(x, c, positions, norm_g, w_ada, b_ada, w_in, conv_w, conv_b, dt_bias, a_log, d_skip,
           ssd_norm_g, b_forget, w_out, final_g):
    bsz, seq, d = x.shape
    depth = w_in.shape[0]
    t = bsz * seq

    mod = _modulation(c, w_ada, b_ada)
    shift = mod[:, :, 0:d].reshape(depth, bsz, 1, d)
    scale = mod[:, :, d:2 * d].reshape(depth, bsz, 1, d)
    gate = mod[:, :, 2 * d:3 * d].reshape(depth, bsz, 1, d)

    cos_t, sin_t = _rope_tables(positions)
    w_main, w_small = _permute_in_weights(w_in)
    w_out_b = w_out.astype(BF16)
    ret_consts = _retention_consts()
    tril = jnp.asarray(np.tril(np.ones((CHUNK, CHUNK), np.float32)), BF16)
    bias_rows = _lane_row(depth, [(LANE_F, b_forget), (LANE_DT, dt_bias)])
    alog_rows = _lane_row(depth, [(LANE_DT, a_log)])
    dskip_rows = jnp.repeat(d_skip, SSD_HEAD_DIM, axis=-1).reshape(depth, 1, SSD_WIDTH)
    zeros_bd = jnp.zeros((bsz, 1, d), F32)

    x2 = x.reshape(t, d)
    h = _first_norm(x2, norm_g[0], scale[0], shift[0], seq)
    out = None
    for l in range(depth):
        proj, small = _in_projection(h, w_main[l], w_small[l])
        ret = _retention(proj, cos_t, sin_t, ret_consts, bsz, seq)
        ssd, f_rows = _ssd(proj, small, conv_w[l], conv_b[l].reshape(1, CONV_DIM), bias_rows[l],
                           alog_rows[l], dskip_rows[l], ssd_norm_g[l].reshape(1, SSD_WIDTH), tril,
                           bsz, seq)
        fox = _fox(proj, f_rows, bsz, seq)
        if l + 1 < depth:
            x2, h = _out_projection(ret, ssd, fox, w_out_b[l], x2, gate[l], norm_g[l + 1],
                                    scale[l + 1], shift[l + 1], seq, BF16, True)
        else:
            (out,) = _out_projection(ret, ssd, fox, w_out_b[l], x2, gate[l], final_g,
                                     zeros_bd, zeros_bd, seq, F32, False)
    return out.reshape(bsz, seq, d)
```

```python
import functools
import math

import numpy as np
import jax
import jax.numpy as jnp
from jax import lax
from jax.experimental import pallas as pl
from jax.experimental.pallas import tpu as pltpu

F32 = jnp.float32
BF16 = jnp.bfloat16

RET_HEADS = 4
RET_HEAD_DIM = 128
RET_WIDTH = RET_HEADS * RET_HEAD_DIM
SSD_HEADS = 16
SSD_HEAD_DIM = 64
SSD_WIDTH = SSD_HEADS * SSD_HEAD_DIM
SSD_GROUPS = 2
SSD_STATE = 128
SSD_HEADS_PER_GROUP = SSD_HEADS // SSD_GROUPS
CONV_K = 4
BC_WIDTH = SSD_GROUPS * SSD_STATE
CONV_DIM = SSD_WIDTH + 2 * BC_WIDTH
FOX_HEADS = 8
FOX_HEAD_DIM = 64
FOX_WIDTH = FOX_HEADS * FOX_HEAD_DIM
FOX_PAIRS = FOX_HEADS // 2
MIX_WIDTH = RET_WIDTH + SSD_WIDTH + FOX_WIDTH
CHUNK = 128
ROPE_BASE = 10000.0
EPS = 1e-6
LOG2E = math.log2(math.e)
FOX_Q_SCALE = FOX_HEAD_DIM ** -0.5 * LOG2E

COL_RET = 0
COL_XS = 4 * RET_WIDTH
COL_Z = COL_XS + SSD_WIDTH
COL_B = COL_Z + SSD_WIDTH
COL_C = COL_B + BC_WIDTH
COL_FOX = COL_C + BC_WIDTH
N_MAIN = COL_FOX + 4 * FOX_WIDTH
SMALL_W = 128
LANE_F = 0
LANE_DT = FOX_HEADS

LANES = 128
TM_PROJ = 2048
TN_PROJ = 1664
TM_OUT = 512
TM_NORM = 1024
TB_RET = 1024
TB_SSD = 512
T_ATT = 512
KV_UNROLL = 4
PRUNE_MARGIN = 152.0
VMEM_LIMIT = 48 * 1024 * 1024
NEG_BIG = -1e30


def _cparams(sem):
    return pltpu.CompilerParams(dimension_semantics=sem, vmem_limit_bytes=VMEM_LIMIT)


def _dot(a, b):
    return jnp.dot(a, b, preferred_element_type=F32)


def _dot_nt(a, b):
    return lax.dot_general(a, b, (((1,), (1,)), ((), ())), preferred_element_type=F32)


def _dot_tn(a, b):
    return lax.dot_general(a, b, (((0,), (0,)), ((), ())), preferred_element_type=F32)


def _silu(t):
    return t * jax.nn.sigmoid(t)


def _split3(t):
    hi = t.astype(BF16)
    r1 = t - hi.astype(F32)
    mid = r1.astype(BF16)
    lo = (r1 - mid.astype(F32)).astype(BF16)
    return hi, mid, lo


def _modnorm(t, g, scale, shift):
    var = jnp.mean(t * t, axis=-1, keepdims=True)
    return t * lax.rsqrt(var + EPS) * g * (1.0 + scale) + shift


def _mod_kernel(c_ref, w_ref, b_ref, o_ref):
    a = _silu(c_ref[...])
    w = w_ref[0]
    a_hi = a.astype(BF16)
    a_lo = (a - a_hi.astype(F32)).astype(BF16)
    w_hi = w.astype(BF16)
    w_lo = (w - w_hi.astype(F32)).astype(BF16)
    o_ref[0] = _dot(a_hi, w_hi) + _dot(a_hi, w_lo) + _dot(a_lo, w_hi) + b_ref[0]


def _modulation(c, w_ada, b_ada):
    depth, d, n3 = w_ada.shape
    bsz = c.shape[0]
    rows = 16
    c_pad = jnp.zeros((rows, d), F32).at[:bsz].set(c)
    tn = 1024
    out = pl.pallas_call(
        _mod_kernel,
        grid=(depth, n3 // tn),
        in_specs=[pl.BlockSpec((rows, d), lambda l, j: (0, 0)),
                  pl.BlockSpec((1, d, tn), lambda l, j: (l, 0, j)),
                  pl.BlockSpec((1, 1, tn), lambda l, j: (l, 0, j))],
        out_specs=pl.BlockSpec((1, rows, tn), lambda l, j: (l, 0, j)),
        out_shape=jax.ShapeDtypeStruct((depth, rows, n3), F32),
        compiler_params=_cparams(("parallel", "parallel")),
        name="adaln_mod",
    )(c_pad, w_ada, b_ada.reshape(depth, 1, n3))
    return out[:, :bsz]


def _rope_kernel(pos_ref, freq_ref, cos_ref, sin_ref):
    pos = pos_ref[0].astype(F32)
    ang = freq_ref[...] * pos
    c = jnp.cos(ang)
    s = jnp.sin(ang)
    cos_ref[...] = jnp.concatenate([c, c], axis=0).T
    sin_ref[...] = jnp.concatenate([-s, s], axis=0).T


def _rope_tables(positions):
    bsz, seq = positions.shape
    half = RET_HEAD_DIM // 2
    freq = (ROPE_BASE ** (-jnp.arange(half, dtype=F32) / half)).reshape(half, 1)
    tm = 512
    nt = seq // tm
    shp = jax.ShapeDtypeStruct((bsz * seq, RET_HEAD_DIM), F32)
    return pl.pallas_call(
        _rope_kernel,
        grid=(bsz, nt),
        in_specs=[pl.BlockSpec((1, 1, tm), lambda b, i: (b, 0, i)),
                  pl.BlockSpec((half, 1), lambda b, i: (0, 0))],
        out_specs=[pl.BlockSpec((tm, RET_HEAD_DIM), lambda b, i: (b * nt + i, 0)),
                   pl.BlockSpec((tm, RET_HEAD_DIM), lambda b, i: (b * nt + i, 0))],
        out_shape=[shp, shp],
        compiler_params=_cparams(("parallel", "parallel")),
        name="rope_tables",
    )(positions.reshape(bsz, 1, seq), freq)


def _norm_kernel(x_ref, g_ref, sc_ref, sh_ref, h_ref):
    h_ref[...] = _modnorm(x_ref[...], g_ref[...], sc_ref[0], sh_ref[0]).astype(h_ref.dtype)


def _first_norm(x2, g, scale, shift, seq):
    t, d = x2.shape
    tm = min(TM_NORM, seq)
    per_b = seq // tm
    return pl.pallas_call(
        _norm_kernel,
        grid=(t // tm,),
        in_specs=[pl.BlockSpec((tm, d), lambda i: (i, 0)),
                  pl.BlockSpec((1, d), lambda i: (0, 0)),
                  pl.BlockSpec((1, 1, d), lambda i: (i // per_b, 0, 0)),
                  pl.BlockSpec((1, 1, d), lambda i: (i // per_b, 0, 0))],
        out_specs=pl.BlockSpec((tm, d), lambda i: (i, 0)),
        out_shape=jax.ShapeDtypeStruct((t, d), BF16),
        compiler_params=_cparams(("parallel",)),
        name="first_norm",
    )(x2, g.reshape(1, d), scale, shift)


def _inproj_kernel(h_ref, w_ref, ws_ref, o_ref, os_ref):
    h = h_ref[...]
    o_ref[...] = _dot(h, w_ref[...]).astype(o_ref.dtype)

    @pl.when(pl.program_id(1) == 0)
    def _():
        os_ref[...] = _dot(h, ws_ref[...])


def _in_projection(h, w_main, w_small):
    t, d = h.shape
    tm = min(TM_PROJ, t)
    return pl.pallas_call(
        _inproj_kernel,
        grid=(t // tm, N_MAIN // TN_PROJ),
        in_specs=[pl.BlockSpec((tm, d), lambda i, j: (i, 0)),
                  pl.BlockSpec((d, TN_PROJ), lambda i, j: (0, j)),
                  pl.BlockSpec((d, SMALL_W), lambda i, j: (0, 0))],
        out_specs=[pl.BlockSpec((tm, TN_PROJ), lambda i, j: (i, j)),
                   pl.BlockSpec((tm, SMALL_W), lambda i, j: (i, 0))],
        out_shape=[jax.ShapeDtypeStruct((t, N_MAIN), BF16),
                   jax.ShapeDtypeStruct((t, SMALL_W), F32)],
        compiler_params=_cparams(("parallel", "arbitrary")),
        name="in_proj",
    )(h, w_main, w_small)


def _retention_consts():
    h = np.arange(RET_HEADS, dtype=np.float64)
    log_g = np.log(1.0 - 2.0 ** (-5.0 - h))
    idx = np.arange(CHUNK, dtype=np.float64)
    diff = idx[:, None] - idx[None, :]
    intra = np.where(diff >= 0, np.exp(log_g[:, None, None] * np.maximum(diff, 0.0)), 0.0)
    dq = np.exp(log_g[:, None] * (idx + 1.0))
    dk = np.exp(log_g[:, None] * (CHUNK - 1.0 - idx))
    dq = np.broadcast_to(dq[:, :, None], (RET_HEADS, CHUNK, RET_HEAD_DIM))
    dk = np.broadcast_to(dk[:, :, None], (RET_HEADS, CHUNK, RET_HEAD_DIM))
    dchunk = tuple(float(v) for v in np.exp(log_g * CHUNK))
    return (jnp.asarray(intra, F32), jnp.asarray(dq, F32), jnp.asarray(dk, F32), dchunk)


def _ret_kernel(q_ref, k_ref, v_ref, g_ref, cos_ref, sin_ref, di_ref, dq_ref, dk_ref,
                o_ref, s_ref, *, nchunk, dchunk):
    @pl.when(pl.program_id(1) == 0)
    def _():
        s_ref[...] = jnp.zeros_like(s_ref)

    kscale = RET_HEAD_DIM ** -0.5

    def body(c, carry):
        rows = pl.ds(pl.multiple_of(c * CHUNK, CHUNK), CHUNK)
        cs = cos_ref[rows, :]
        sn = sin_ref[rows, :]
        for h in range(RET_HEADS):
            cols = slice(h * RET_HEAD_DIM, (h + 1) * RET_HEAD_DIM)
            q = q_ref[rows, cols].astype(F32)
            k = k_ref[rows, cols].astype(F32)
            v = v_ref[rows, cols]
            qr = q * cs + pltpu.roll(q, RET_HEAD_DIM // 2, 1) * sn
            kr = (k * cs + pltpu.roll(k, RET_HEAD_DIM // 2, 1) * sn) * kscale
            qb = qr.astype(BF16)
            kb = kr.astype(BF16)
            state = s_ref[h]
            s = _dot_nt(qb, kb) * di_ref[h]
            o = _dot(s.astype(BF16), v) + _dot(qb, state.astype(BF16)) * dq_ref[h]
            kd = (kr * dk_ref[h]).astype(BF16)
            s_ref[h] = state * dchunk[h] + _dot_tn(kd, v)
            o = o * lax.rsqrt(jnp.mean(o * o, axis=-1, keepdims=True) + EPS)
            o = o * _silu(g_ref[rows, cols].astype(F32))
            o_ref[rows, cols] = o.astype(o_ref.dtype)
        return carry

    lax.fori_loop(0, nchunk, body, 0, unroll=4)


def _retention(proj, cos_t, sin_t, consts, bsz, seq):
    t = proj.shape[0]
    tb = min(TB_RET, seq)
    per_b = seq // tb
    intra, dq, dk, dchunk = consts
    row = lambda b, j: b * per_b + j
    cblk = lambda idx: pl.BlockSpec((tb, RET_WIDTH), lambda b, j: (row(b, j), idx))
    const3 = pl.BlockSpec((RET_HEADS, CHUNK, RET_HEAD_DIM), lambda b, j: (0, 0, 0))
    tblk = pl.BlockSpec((tb, RET_HEAD_DIM), lambda b, j: (row(b, j), 0))
    base = COL_RET // RET_WIDTH
    return pl.pallas_call(
        functools.partial(_ret_kernel, nchunk=tb // CHUNK, dchunk=dchunk),
        grid=(bsz, per_b),
        in_specs=[cblk(base), cblk(base + 1), cblk(base + 2), cblk(base + 3),
                  tblk, tblk, const3, const3, const3],
        out_specs=pl.BlockSpec((tb, RET_WIDTH), lambda b, j: (row(b, j), 0)),
        out_shape=jax.ShapeDtypeStruct((t, RET_WIDTH), BF16),
        scratch_shapes=[pltpu.VMEM((RET_HEADS, RET_HEAD_DIM, RET_HEAD_DIM), F32)],
        compiler_params=_cparams(("parallel", "arbitrary")),
        name="retention",
    )(proj, proj, proj, proj, cos_t, sin_t, intra, dq, dk)


def _ssd_kernel(xs_ref, z_ref, b_ref, c_ref, sm_ref, cw_ref, cb_ref, bias_ref, alog_ref,
                dskip_ref, ng_ref, tril_ref, shift_ref, y_ref, f_ref,
                ubuf, st_ref, ybuf, fcar, xcf, xsb, inter, *, tb):
    j = pl.program_id(1)

    @pl.when(j == 0)
    def _():
        ubuf[0:CHUNK, :] = jnp.zeros((CHUNK, CONV_DIM), BF16)
        st_ref[...] = jnp.zeros_like(st_ref)
        fcar[...] = jnp.zeros_like(fcar)

    @pl.when(j > 0)
    def _():
        ubuf[0:CHUNK, :] = ubuf[tb:tb + CHUNK, :]

    ubuf[CHUNK:CHUNK + tb, 0:SSD_WIDTH] = xs_ref[...]
    ubuf[CHUNK:CHUNK + tb, SSD_WIDTH:SSD_WIDTH + BC_WIDTH] = b_ref[...]
    ubuf[CHUNK:CHUNK + tb, SSD_WIDTH + BC_WIDTH:CONV_DIM] = c_ref[...]

    slab = 2 * LANES

    def conv_silu(c):
        for s0 in range(0, CONV_DIM, slab):
            cols = slice(s0, s0 + slab)
            win = ubuf[c * CHUNK:(c + 2) * CHUNK, cols]
            acc = cb_ref[:, cols] + cw_ref[CONV_K - 1:CONV_K, cols] * win[CHUNK:2 * CHUNK, :].astype(F32)
            for sh in range(1, CONV_K):
                tap = CONV_K - 1 - sh
                acc = acc + cw_ref[tap:tap + 1, cols] * _dot(shift_ref[sh - 1], win)
            xc = _silu(acc)
            xcf[:, cols] = xc
            if s0 < SSD_WIDTH:
                xsb[:, cols] = xc.astype(BF16)

    lane = lax.broadcasted_iota(jnp.int32, (1, LANES), 1)
    is_f = lane < LANE_DT
    half_lane = lane < SSD_HEAD_DIM
    a_row = jnp.where((lane >= LANE_DT) & (lane < LANE_DT + SSD_HEADS), -jnp.exp(alog_ref[...]), 0.0)
    ri = lax.broadcasted_iota(jnp.int32, (CHUNK, CHUNK), 0)
    ci = lax.broadcasted_iota(jnp.int32, (CHUNK, CHUNK), 1)
    causal = ri >= ci
    tril = tril_ref[...]

    for c in range(tb // CHUNK):
        r0 = c * CHUNK
        conv_silu(c)

        pre = sm_ref[r0:r0 + CHUNK, :] + bias_ref[...]
        tail_term = jnp.log1p(jnp.exp(-jnp.abs(pre)))
        sp = jnp.maximum(pre, 0.0) + tail_term
        val = jnp.where(is_f, -(jnp.maximum(-pre, 0.0) + tail_term), sp * a_row)
        v_hi, v_mid, v_lo = _split3(val)
        cum = _dot(tril, v_hi) + _dot(tril, v_mid) + _dot(tril, v_lo)
        cum = cum + fcar[...]
        fcar[...] = jnp.where(is_f, cum[CHUNK - 1:CHUNK, :], 0.0)
        cum_t = cum.T
        dt_t = sp.T
        n_hi, n_mid, n_lo = _split3(jnp.where(is_f, cum * -LOG2E, 0.0))
        f_ref[r0:r0 + CHUNK, :] = (n_hi.astype(F32) + pltpu.roll(n_mid.astype(F32), FOX_HEADS, 1)
                                   + pltpu.roll(n_lo.astype(F32), 2 * FOX_HEADS, 1)).astype(BF16)
        e_col = jnp.exp(cum)

        for g in range(SSD_GROUPS):
            bcol = SSD_WIDTH + g * SSD_STATE
            bm = xcf[:, bcol:bcol + SSD_STATE]
            cm_b = xcf[:, bcol + BC_WIDTH:bcol + BC_WIDTH + SSD_STATE].astype(BF16)
            cb = _dot_nt(cm_b, bm.astype(BF16))
            bm_t = bm.T
            inter[...] = _dot(cm_b, st_ref[g].astype(BF16))
            for pr in range(SSD_HEADS_PER_GROUP // 2):
                gp = g * (SSD_HEADS_PER_GROUP // 2) + pr
                xp_b = xsb[:, gp * LANES:(gp + 1) * LANES]
                ys = []
                ups = []
                ecs = []
                els = []
                for hh in range(2):
                    ln = LANE_DT + 2 * gp + hh
                    a_c = cum[:, ln:ln + 1]
                    a_r = cum_t[ln:ln + 1, :]
                    d_r = dt_t[ln:ln + 1, :]
                    lm = jnp.where(causal, jnp.exp(a_c - a_r), 0.0)
                    m = cb * lm * d_r
                    ys.append(_dot(m.astype(BF16), xp_b))
                    last = a_r[:, CHUNK - 1:CHUNK]
                    w_r = d_r * jnp.exp(last - a_r)
                    ups.append(_dot((bm_t * w_r).astype(BF16), xp_b))
                    ecs.append(e_col[:, ln:ln + 1])
                    els.append(jnp.exp(last))
                lanes = slice(pr * LANES, (pr + 1) * LANES)
                y_pair = (jnp.where(half_lane, ys[0], ys[1])
                          + jnp.where(half_lane, ecs[0], ecs[1]) * inter[:, lanes])
                ybuf[:, gp * LANES:(gp + 1) * LANES] = y_pair
                st_ref[g, :, lanes] = (st_ref[g, :, lanes] * jnp.where(half_lane, els[0], els[1])
                                       + jnp.where(half_lane, ups[0], ups[1]))

        ss = jnp.zeros((CHUNK, 1), F32)
        for s0 in range(0, SSD_WIDTH, slab):
            cols = slice(s0, s0 + slab)
            gated = ((ybuf[:, cols] + dskip_ref[:, cols] * xcf[:, cols])
                     * _silu(z_ref[r0:r0 + CHUNK, cols].astype(F32)))
            ybuf[:, cols] = gated
            ss = ss + jnp.sum(gated * gated, axis=-1, keepdims=True)
        inv = lax.rsqrt(ss * (1.0 / SSD_WIDTH) + EPS)
        for s0 in range(0, SSD_WIDTH, slab):
            cols = slice(s0, s0 + slab)
            y_ref[r0:r0 + CHUNK, cols] = (ybuf[:, cols] * inv * ng_ref[:, cols]).astype(y_ref.dtype)


def _shift_matrices():
    m = np.zeros((CONV_K - 1, CHUNK, 2 * CHUNK), np.float32)
    t = np.arange(CHUNK)
    for sh in range(1, CONV_K):
        m[sh - 1, t, CHUNK + t - sh] = 1.0
    return jnp.asarray(m, BF16)


def _ssd(proj, small, conv_w, conv_b, bias_row, alog_row, dskip_row, norm_g, tril, shifts, bsz, seq):
    t = proj.shape[0]
    tb = min(TB_SSD, seq)
    per_b = seq // tb
    row = lambda b, j: b * per_b + j
    full = lambda shape: pl.BlockSpec(shape, lambda b, j: (0,) * len(shape))
    return pl.pallas_call(
        functools.partial(_ssd_kernel, tb=tb),
        grid=(bsz, per_b),
        in_specs=[pl.BlockSpec((tb, SSD_WIDTH), lambda b, j: (row(b, j), COL_XS // SSD_WIDTH)),
                  pl.BlockSpec((tb, SSD_WIDTH), lambda b, j: (row(b, j), COL_Z // SSD_WIDTH)),
                  pl.BlockSpec((tb, BC_WIDTH), lambda b, j: (row(b, j), COL_B // BC_WIDTH)),
                  pl.BlockSpec((tb, BC_WIDTH), lambda b, j: (row(b, j), COL_C // BC_WIDTH)),
                  pl.BlockSpec((tb, SMALL_W), lambda b, j: (row(b, j), 0)),
                  full((CONV_K, CONV_DIM)), full((1, CONV_DIM)), full((1, SMALL_W)),
                  full((1, SMALL_W)), full((1, SSD_WIDTH)), full((1, SSD_WIDTH)),
                  full((CHUNK, CHUNK)), full((CONV_K - 1, CHUNK, 2 * CHUNK))],
        out_specs=[pl.BlockSpec((tb, SSD_WIDTH), lambda b, j: (row(b, j), 0)),
                   pl.BlockSpec((tb, SMALL_W), lambda b, j: (row(b, j), 0))],
        out_shape=[jax.ShapeDtypeStruct((t, SSD_WIDTH), BF16),
                   jax.ShapeDtypeStruct((t, SMALL_W), BF16)],
        scratch_shapes=[pltpu.VMEM((CHUNK + tb, CONV_DIM), BF16),
                        pltpu.VMEM((SSD_GROUPS, SSD_STATE, SSD_HEADS_PER_GROUP * SSD_HEAD_DIM), F32),
                        pltpu.VMEM((CHUNK, SSD_WIDTH), F32),
                        pltpu.VMEM((1, SMALL_W), F32),
                        pltpu.VMEM((CHUNK, CONV_DIM), F32),
                        pltpu.VMEM((CHUNK, SSD_WIDTH), BF16),
                        pltpu.VMEM((CHUNK, SSD_HEADS_PER_GROUP * SSD_HEAD_DIM), F32)],
        compiler_params=_cparams(("parallel", "arbitrary")),
        name="ssd",
    )(proj, proj, proj, proj, small, conv_w, conv_b, bias_row, alog_row, dskip_row, norm_g, tril,
      shifts)


def _fox_kernel(nf_ref, q_ref, k_ref, v_ref, g_ref, f_ref, o_ref, kaug, vaug, m_ref, acc_ref, knorm,
                *, tq, nq):
    bat = pl.program_id(0)
    pair = pl.program_id(1)
    i = pl.program_id(2)
    lane = lax.broadcasted_iota(jnp.int32, (1, LANES), 1)
    head0 = lane < FOX_HEAD_DIM

    def head_norms(t):
        t2 = t.astype(F32) * t.astype(F32)
        n0 = jnp.sum(jnp.where(head0, t2, 0.0), axis=1, keepdims=True)
        n1 = jnp.sum(jnp.where(head0, 0.0, t2), axis=1, keepdims=True)
        return jnp.sqrt(jnp.max(n0)), jnp.sqrt(jnp.max(n1))

    @pl.when(i == 0)
    def _():
        kaug[:, 0:LANES] = k_ref[...]
        kaug[:, LANES:2 * LANES] = f_ref[...]
        vaug[:, 0:LANES] = v_ref[...]
        vaug[:, LANES:2 * LANES] = jnp.ones((vaug.shape[0], LANES), BF16)
        knorm[0], knorm[1] = head_norms(k_ref[...])

    q = q_ref[...]
    zero = jnp.zeros_like(q)
    qnorm = head_norms(q)
    qh = []
    starts = []
    for h in range(2):
        hd = 2 * pair + h
        sel = (lane == hd) | (lane == hd + FOX_HEADS) | (lane == hd + 2 * FOX_HEADS)
        ones = jnp.broadcast_to(jnp.where(sel, 1.0, 0.0).astype(BF16), (tq, LANES))
        qm = jnp.where(head0, q, zero) if h == 0 else jnp.where(head0, zero, q)
        qh.append(jnp.concatenate([qm, ones], axis=1))
        row0 = (bat * nq) * FOX_HEADS + hd
        thresh = (nf_ref[row0 + jnp.maximum(i - 1, 0) * FOX_HEADS]
                  - 2.0 * qnorm[h] * knorm[h] - PRUNE_MARGIN)
        starts.append(lax.fori_loop(
            0, i, lambda kb, c: c + (nf_ref[row0 + kb * FOX_HEADS] <= thresh).astype(jnp.int32), 0))
    start = jnp.minimum(starts[0], starts[1])

    m_ref[...] = jnp.full(m_ref.shape, NEG_BIG, F32)
    acc_ref[...] = jnp.zeros_like(acc_ref)

    def block(kb, masked):
        cols = pl.ds(pl.multiple_of(kb * tq, tq), tq)
        k = kaug[cols, :]
        v = vaug[cols, :]
        if masked:
            ri = lax.broadcasted_iota(jnp.int32, (tq, tq), 0)
            ci = lax.broadcasted_iota(jnp.int32, (tq, tq), 1)
            keep = ri >= ci
        for h in range(2):
            s = _dot_nt(qh[h], k)
            if masked:
                s = jnp.where(keep, s, NEG_BIG)
            m_old = m_ref[h]
            m_new = jnp.maximum(m_old, jnp.max(s, axis=-1, keepdims=True))
            alpha = jnp.exp2(m_old - m_new)
            p = jnp.exp2(s - jnp.concatenate([m_new] * (tq // LANES), axis=1))
            acc_ref[h] = (jnp.concatenate([alpha, alpha], axis=1) * acc_ref[h]
                          + _dot(p.astype(BF16), v))
            m_ref[h] = m_new

    def body(kq, carry):
        for u in range(KV_UNROLL):
            block(start + KV_UNROLL * kq + u, False)
        return carry

    full = (i - start) // KV_UNROLL
    lax.fori_loop(0, full, body, 0)
    tail0 = start + full * KV_UNROLL
    for extra in range(KV_UNROLL):
        @pl.when(i - tail0 == extra)
        def _(extra=extra):
            for u in range(extra):
                block(tail0 + u, False)
            block(i, True)

    o = jnp.where(head0, acc_ref[0, :, 0:LANES] / acc_ref[0, :, LANES:2 * LANES],
                  acc_ref[1, :, 0:LANES] / acc_ref[1, :, LANES:2 * LANES])
    o = o * _silu(g_ref[...].astype(F32))
    o_ref[...] = o.astype(o_ref.dtype)


def _fox(proj, f_aug, bsz, seq):
    t = proj.shape[0]
    tq = min(T_ATT, seq)
    nq = seq // tq
    base = COL_FOX // LANES
    nblk = FOX_WIDTH // LANES
    ends = f_aug.reshape(bsz, nq, tq, SMALL_W)[:, :, tq - 1, 0:3 * FOX_HEADS].astype(F32)
    nf_ends = ends.reshape(bsz, nq, 3, FOX_HEADS).sum(axis=2).reshape(-1)
    return pl.pallas_call(
        functools.partial(_fox_kernel, tq=tq, nq=nq),
        grid=(bsz, FOX_PAIRS, nq),
        in_specs=[pl.BlockSpec(memory_space=pltpu.SMEM),
                  pl.BlockSpec((tq, LANES), lambda b, p, i: (b * nq + i, base + p)),
                  pl.BlockSpec((seq, LANES), lambda b, p, i: (b, base + nblk + p)),
                  pl.BlockSpec((seq, LANES), lambda b, p, i: (b, base + 2 * nblk + p)),
                  pl.BlockSpec((tq, LANES), lambda b, p, i: (b * nq + i, base + 3 * nblk + p)),
                  pl.BlockSpec((seq, SMALL_W), lambda b, p, i: (b, 0))],
        out_specs=pl.BlockSpec((tq, LANES), lambda b, p, i: (b * nq + i, p)),
        out_shape=jax.ShapeDtypeStruct((t, FOX_WIDTH), BF16),
        scratch_shapes=[pltpu.VMEM((seq, 2 * LANES), BF16),
                        pltpu.VMEM((seq, 2 * LANES), BF16),
                        pltpu.VMEM((2, tq, LANES), F32),
                        pltpu.VMEM((2, tq, 2 * LANES), F32),
                        pltpu.SMEM((2,), F32)],
        compiler_params=_cparams(("parallel", "parallel", "arbitrary")),
        name="fox_attention",
    )(nf_ends, proj, proj, proj, proj, f_aug)


def _out_kernel(ret_ref, ssd_ref, fox_ref, w_ref, x_ref, gate_ref, g_ref, sc_ref, sh_ref,
                *out_refs, emit_x):
    acc = _dot(ret_ref[...], w_ref[0:RET_WIDTH, :])
    acc = acc + _dot(ssd_ref[...], w_ref[RET_WIDTH:RET_WIDTH + SSD_WIDTH, :])
    acc = acc + _dot(fox_ref[...], w_ref[RET_WIDTH + SSD_WIDTH:MIX_WIDTH, :])
    xn = x_ref[...] + gate_ref[0] * acc
    if emit_x:
        out_refs[0][...] = xn
    y_ref = out_refs[-1]
    y_ref[...] = _modnorm(xn, g_ref[...], sc_ref[0], sh_ref[0]).astype(y_ref.dtype)


def _out_projection(ret, ssd, fox, w_out, x2, gate, g, scale, shift, seq, y_dtype, emit_x):
    t, d = x2.shape
    tm = min(TM_OUT, seq)
    per_b = seq // tm
    rowblk = lambda w: pl.BlockSpec((tm, w), lambda i: (i, 0))
    perb = pl.BlockSpec((1, 1, d), lambda i: (i // per_b, 0, 0))
    out_specs = [rowblk(d)]
    out_shape = [jax.ShapeDtypeStruct((t, d), y_dtype)]
    if emit_x:
        out_specs = [rowblk(d)] + out_specs
        out_shape = [jax.ShapeDtypeStruct((t, d), F32)] + out_shape
    return pl.pallas_call(
        functools.partial(_out_kernel, emit_x=emit_x),
        grid=(t // tm,),
        in_specs=[rowblk(RET_WIDTH), rowblk(SSD_WIDTH), rowblk(FOX_WIDTH),
                  pl.BlockSpec((MIX_WIDTH, d), lambda i: (0, 0)),
                  rowblk(d), perb, pl.BlockSpec((1, d), lambda i: (0, 0)), perb, perb],
        out_specs=out_specs,
        out_shape=out_shape,
        compiler_params=_cparams(("parallel",)),
        name="out_proj",
    )(ret, ssd, fox, w_out, x2, gate, g.reshape(1, d), scale, shift)


def _permute_in_weights(w_in):
    o_xbc = 4 * RET_WIDTH
    o_dt = o_xbc + CONV_DIM
    o_z = o_dt + SSD_HEADS
    o_fox = o_z + SSD_WIDTH
    o_f = o_fox + 4 * FOX_WIDTH
    main = jnp.concatenate([
        w_in[..., 0:o_xbc],
        w_in[..., o_xbc:o_xbc + SSD_WIDTH],
        w_in[..., o_z:o_z + SSD_WIDTH],
        w_in[..., o_xbc + SSD_WIDTH:o_dt],
        w_in[..., o_fox:o_fox + FOX_WIDTH] * FOX_Q_SCALE,
        w_in[..., o_fox + FOX_WIDTH:o_f],
    ], axis=-1).astype(BF16)
    pad = jnp.zeros(w_in.shape[:-1] + (SMALL_W - FOX_HEADS - SSD_HEADS,), w_in.dtype)
    small = jnp.concatenate([w_in[..., o_f:o_f + FOX_HEADS], w_in[..., o_dt:o_z], pad],
                            axis=-1).astype(BF16)
    return main, small


def _lane_row(depth, pieces):
    row = jnp.zeros((depth, 1, SMALL_W), F32)
    for start, arr in pieces:
        row = row.at[:, 0, start:start + arr.shape[-1]].set(arr.astype(F32))
    return row


def kernel(x, c, positions, norm_g, w_ada, b_ada, w_in, conv_w, conv_b, dt_bias, a_log, d_skip,
           ssd_norm_g, b_forget, w_out, final_g):
    bsz, seq, d = x.shape
    depth = w_in.shape[0]
    t = bsz * seq

    mod = _modulation(c, w_ada, b_ada)
    shift = mod[:, :, 0:d].reshape(depth, bsz, 1, d)
    scale = mod[:, :, d:2 * d].reshape(depth, bsz, 1, d)
    gate = mod[:, :, 2 * d:3 * d].reshape(depth, bsz, 1, d)

    cos_t, sin_t = _rope_tables(positions)
    w_main, w_small = _permute_in_weights(w_in)
    w_out_b = w_out.astype(BF16)
    ret_consts = _retention_consts()
    tril = jnp.asarray(np.tril(np.ones((CHUNK, CHUNK), np.float32)), BF16)
    shifts = _shift_matrices()
    bias_rows = _lane_row(depth, [(LANE_F, b_forget), (LANE_DT, dt_bias)])
    alog_rows = _lane_row(depth, [(LANE_DT, a_log)])
    dskip_rows = jnp.repeat(d_skip, SSD_HEAD_DIM, axis=-1).reshape(depth, 1, SSD_WIDTH)
    zeros_bd = jnp.zeros((bsz, 1, d), F32)

    x2 = x.reshape(t, d)
    h = _first_norm(x2, norm_g[0], scale[0], shift[0], seq)
    out = None
    for l in range(depth):
        proj, small = _in_projection(h, w_main[l], w_small[l])
        ret = _retention(proj, cos_t, sin_t, ret_consts, bsz, seq)
        ssd, f_rows = _ssd(proj, small, conv_w[l], conv_b[l].reshape(1, CONV_DIM), bias_rows[l],
                           alog_rows[l], dskip_rows[l], ssd_norm_g[l].reshape(1, SSD_WIDTH), tril,
                           shifts, bsz, seq)
        fox = _fox(proj, f_rows, bsz, seq)
        if l + 1 < depth:
            x2, h = _out_projection(ret, ssd, fox, w_out_b[l], x2, gate[l], norm_g[l + 1],
                                    scale[l + 1], shift[l + 1], seq, BF16, True)
        else:
            (out,) = _out_projection(ret, ssd, fox, w_out_b[l], x2, gate[l], final_g,
                                     zeros_bd, zeros_bd, seq, F32, False)
    return out.reshape(bsz, seq, d)
```

```python
import functools
import math

import numpy as np
import jax
import jax.numpy as jnp
from jax import lax
from jax.experimental import pallas as pl
from jax.experimental.pallas import tpu as pltpu

F32 = jnp.float32
BF16 = jnp.bfloat16

RET_HEADS = 4
RET_HEAD_DIM = 128
RET_WIDTH = RET_HEADS * RET_HEAD_DIM
SSD_HEADS = 16
SSD_HEAD_DIM = 64
SSD_WIDTH = SSD_HEADS * SSD_HEAD_DIM
SSD_GROUPS = 2
SSD_STATE = 128
SSD_HEADS_PER_GROUP = SSD_HEADS // SSD_GROUPS
CONV_K = 4
BC_WIDTH = SSD_GROUPS * SSD_STATE
CONV_DIM = SSD_WIDTH + 2 * BC_WIDTH
FOX_HEADS = 8
FOX_HEAD_DIM = 64
FOX_WIDTH = FOX_HEADS * FOX_HEAD_DIM
FOX_PAIRS = FOX_HEADS // 2
MIX_WIDTH = RET_WIDTH + SSD_WIDTH + FOX_WIDTH
CHUNK = 128
ROPE_BASE = 10000.0
EPS = 1e-6
LOG2E = math.log2(math.e)
FOX_Q_SCALE = FOX_HEAD_DIM ** -0.5 * LOG2E

COL_RET = 0
COL_XS = 4 * RET_WIDTH
COL_Z = COL_XS + SSD_WIDTH
COL_B = COL_Z + SSD_WIDTH
COL_C = COL_B + BC_WIDTH
COL_FOX = COL_C + BC_WIDTH
N_MAIN = COL_FOX + 4 * FOX_WIDTH
SMALL_W = 128
LANE_F = 0
LANE_DT = FOX_HEADS

LANES = 128
TM_PROJ = 2048
TN_PROJ = 1664
TM_OUT = 512
TM_NORM = 1024
TB_RET = 1024
TB_SSD = 512
T_ATT = 512
KV_UNROLL = 4
PRUNE_MARGIN = 152.0
NORM_SLACK = 1.02
VMEM_LIMIT = 48 * 1024 * 1024
NEG_BIG = -1e30


def _cparams(sem):
    return pltpu.CompilerParams(dimension_semantics=sem, vmem_limit_bytes=VMEM_LIMIT)


def _dot(a, b):
    return jnp.dot(a, b, preferred_element_type=F32)


def _dot_nt(a, b):
    return lax.dot_general(a, b, (((1,), (1,)), ((), ())), preferred_element_type=F32)


def _dot_tn(a, b):
    return lax.dot_general(a, b, (((0,), (0,)), ((), ())), preferred_element_type=F32)


def _silu(t):
    return t * jax.nn.sigmoid(t)


def _split3(t):
    hi = t.astype(BF16)
    r1 = t - hi.astype(F32)
    mid = r1.astype(BF16)
    lo = (r1 - mid.astype(F32)).astype(BF16)
    return hi, mid, lo


def _modnorm(t, g, scale, shift):
    var = jnp.mean(t * t, axis=-1, keepdims=True)
    return t * lax.rsqrt(var + EPS) * g * (1.0 + scale) + shift


def _mod_kernel(c_ref, w_ref, b_ref, o_ref):
    a = _silu(c_ref[...])
    w = w_ref[0]
    a_hi = a.astype(BF16)
    a_lo = (a - a_hi.astype(F32)).astype(BF16)
    w_hi = w.astype(BF16)
    w_lo = (w - w_hi.astype(F32)).astype(BF16)
    o_ref[0] = _dot(a_hi, w_hi) + _dot(a_hi, w_lo) + _dot(a_lo, w_hi) + b_ref[0]


def _modulation(c, w_ada, b_ada):
    depth, d, n3 = w_ada.shape
    bsz = c.shape[0]
    rows = 16
    c_pad = jnp.zeros((rows, d), F32).at[:bsz].set(c)
    tn = 1024
    out = pl.pallas_call(
        _mod_kernel,
        grid=(depth, n3 // tn),
        in_specs=[pl.BlockSpec((rows, d), lambda l, j: (0, 0)),
                  pl.BlockSpec((1, d, tn), lambda l, j: (l, 0, j)),
                  pl.BlockSpec((1, 1, tn), lambda l, j: (l, 0, j))],
        out_specs=pl.BlockSpec((1, rows, tn), lambda l, j: (l, 0, j)),
        out_shape=jax.ShapeDtypeStruct((depth, rows, n3), F32),
        compiler_params=_cparams(("parallel", "parallel")),
        name="adaln_mod",
    )(c_pad, w_ada, b_ada.reshape(depth, 1, n3))
    return out[:, :bsz]


def _rope_kernel(pos_ref, freq_ref, cos_ref, sin_ref):
    pos = pos_ref[0].astype(F32)
    ang = freq_ref[...] * pos
    c = jnp.cos(ang)
    s = jnp.sin(ang)
    cos_ref[...] = jnp.concatenate([c, c], axis=0).T
    sin_ref[...] = jnp.concatenate([-s, s], axis=0).T


def _rope_tables(positions):
    bsz, seq = positions.shape
    half = RET_HEAD_DIM // 2
    freq = (ROPE_BASE ** (-jnp.arange(half, dtype=F32) / half)).reshape(half, 1)
    tm = 512
    nt = seq // tm
    shp = jax.ShapeDtypeStruct((bsz * seq, RET_HEAD_DIM), F32)
    return pl.pallas_call(
        _rope_kernel,
        grid=(bsz, nt),
        in_specs=[pl.BlockSpec((1, 1, tm), lambda b, i: (b, 0, i)),
                  pl.BlockSpec((half, 1), lambda b, i: (0, 0))],
        out_specs=[pl.BlockSpec((tm, RET_HEAD_DIM), lambda b, i: (b * nt + i, 0)),
                   pl.BlockSpec((tm, RET_HEAD_DIM), lambda b, i: (b * nt + i, 0))],
        out_shape=[shp, shp],
        compiler_params=_cparams(("parallel", "parallel")),
        name="rope_tables",
    )(positions.reshape(bsz, 1, seq), freq)


def _norm_kernel(x_ref, g_ref, sc_ref, sh_ref, h_ref):
    h_ref[...] = _modnorm(x_ref[...], g_ref[...], sc_ref[0], sh_ref[0]).astype(h_ref.dtype)


def _first_norm(x2, g, scale, shift, seq):
    t, d = x2.shape
    tm = min(TM_NORM, seq)
    per_b = seq // tm
    return pl.pallas_call(
        _norm_kernel,
        grid=(t // tm,),
        in_specs=[pl.BlockSpec((tm, d), lambda i: (i, 0)),
                  pl.BlockSpec((1, d), lambda i: (0, 0)),
                  pl.BlockSpec((1, 1, d), lambda i: (i // per_b, 0, 0)),
                  pl.BlockSpec((1, 1, d), lambda i: (i // per_b, 0, 0))],
        out_specs=pl.BlockSpec((tm, d), lambda i: (i, 0)),
        out_shape=jax.ShapeDtypeStruct((t, d), BF16),
        compiler_params=_cparams(("parallel",)),
        name="first_norm",
    )(x2, g.reshape(1, d), scale, shift)


def _inproj_kernel(h_ref, w_ref, ws_ref, o_ref, os_ref):
    h = h_ref[...]
    o_ref[...] = _dot(h, w_ref[...]).astype(o_ref.dtype)

    @pl.when(pl.program_id(1) == 0)
    def _():
        os_ref[...] = _dot(h, ws_ref[...])


def _in_projection(h, w_main, w_small):
    t, d = h.shape
    tm = min(TM_PROJ, t)
    return pl.pallas_call(
        _inproj_kernel,
        grid=(t // tm, N_MAIN // TN_PROJ),
        in_specs=[pl.BlockSpec((tm, d), lambda i, j: (i, 0)),
                  pl.BlockSpec((d, TN_PROJ), lambda i, j: (0, j)),
                  pl.BlockSpec((d, SMALL_W), lambda i, j: (0, 0))],
        out_specs=[pl.BlockSpec((tm, TN_PROJ), lambda i, j: (i, j)),
                   pl.BlockSpec((tm, SMALL_W), lambda i, j: (i, 0))],
        out_shape=[jax.ShapeDtypeStruct((t, N_MAIN), BF16),
                   jax.ShapeDtypeStruct((t, SMALL_W), F32)],
        compiler_params=_cparams(("parallel", "arbitrary")),
        name="in_proj",
    )(h, w_main, w_small)


def _retention_consts():
    h = np.arange(RET_HEADS, dtype=np.float64)
    log_g = np.log(1.0 - 2.0 ** (-5.0 - h))
    idx = np.arange(CHUNK, dtype=np.float64)
    diff = idx[:, None] - idx[None, :]
    intra = np.where(diff >= 0, np.exp(log_g[:, None, None] * np.maximum(diff, 0.0)), 0.0)
    dq = np.exp(log_g[:, None] * (idx + 1.0))
    dk = np.exp(log_g[:, None] * (CHUNK - 1.0 - idx))
    dq = np.broadcast_to(dq[:, :, None], (RET_HEADS, CHUNK, RET_HEAD_DIM))
    dk = np.broadcast_to(dk[:, :, None], (RET_HEADS, CHUNK, RET_HEAD_DIM))
    dchunk = tuple(float(v) for v in np.exp(log_g * CHUNK))
    return (jnp.asarray(intra, F32), jnp.asarray(dq, F32), jnp.asarray(dk, F32), dchunk)


def _ret_kernel(q_ref, k_ref, v_ref, g_ref, cos_ref, sin_ref, di_ref, dq_ref, dk_ref,
                o_ref, s_ref, *, nchunk, dchunk):
    @pl.when(pl.program_id(1) == 0)
    def _():
        s_ref[...] = jnp.zeros_like(s_ref)

    kscale = RET_HEAD_DIM ** -0.5

    def body(c, carry):
        rows = pl.ds(pl.multiple_of(c * CHUNK, CHUNK), CHUNK)
        cs = cos_ref[rows, :]
        sn = sin_ref[rows, :]
        for h in range(RET_HEADS):
            cols = slice(h * RET_HEAD_DIM, (h + 1) * RET_HEAD_DIM)
            q = q_ref[rows, cols].astype(F32)
            k = k_ref[rows, cols].astype(F32)
            v = v_ref[rows, cols]
            qr = q * cs + pltpu.roll(q, RET_HEAD_DIM // 2, 1) * sn
            kr = (k * cs + pltpu.roll(k, RET_HEAD_DIM // 2, 1) * sn) * kscale
            qb = qr.astype(BF16)
            kb = kr.astype(BF16)
            state = s_ref[h]
            s = _dot_nt(qb, kb) * di_ref[h]
            o = _dot(s.astype(BF16), v) + _dot(qb, state.astype(BF16)) * dq_ref[h]
            kd = (kr * dk_ref[h]).astype(BF16)
            s_ref[h] = state * dchunk[h] + _dot_tn(kd, v)
            o = o * lax.rsqrt(jnp.mean(o * o, axis=-1, keepdims=True) + EPS)
            o = o * _silu(g_ref[rows, cols].astype(F32))
            o_ref[rows, cols] = o.astype(o_ref.dtype)
        return carry

    lax.fori_loop(0, nchunk, body, 0, unroll=4)


def _retention(proj, cos_t, sin_t, consts, bsz, seq):
    t = proj.shape[0]
    tb = min(TB_RET, seq)
    per_b = seq // tb
    intra, dq, dk, dchunk = consts
    row = lambda b, j: b * per_b + j
    cblk = lambda idx: pl.BlockSpec((tb, RET_WIDTH), lambda b, j: (row(b, j), idx))
    const3 = pl.BlockSpec((RET_HEADS, CHUNK, RET_HEAD_DIM), lambda b, j: (0, 0, 0))
    tblk = pl.BlockSpec((tb, RET_HEAD_DIM), lambda b, j: (row(b, j), 0))
    base = COL_RET // RET_WIDTH
    return pl.pallas_call(
        functools.partial(_ret_kernel, nchunk=tb // CHUNK, dchunk=dchunk),
        grid=(bsz, per_b),
        in_specs=[cblk(base), cblk(base + 1), cblk(base + 2), cblk(base + 3),
                  tblk, tblk, const3, const3, const3],
        out_specs=pl.BlockSpec((tb, RET_WIDTH), lambda b, j: (row(b, j), 0)),
        out_shape=jax.ShapeDtypeStruct((t, RET_WIDTH), BF16),
        scratch_shapes=[pltpu.VMEM((RET_HEADS, RET_HEAD_DIM, RET_HEAD_DIM), F32)],
        compiler_params=_cparams(("parallel", "arbitrary")),
        name="retention",
    )(proj, proj, proj, proj, cos_t, sin_t, intra, dq, dk)


def _ssd_kernel(xs_ref, z_ref, b_ref, c_ref, sm_ref, cw_ref, cb_ref, bias_ref, alog_ref,
                dskip_ref, ng_ref, tril_ref, shift_ref, y_ref, f_ref,
                ubuf, st_ref, ybuf2, fcar, xcf2, xsb2, inter2, *, tb):
    j = pl.program_id(1)

    @pl.when(j == 0)
    def _():
        ubuf[0:CHUNK, :] = jnp.zeros((CHUNK, CONV_DIM), BF16)
        st_ref[...] = jnp.zeros_like(st_ref)
        fcar[...] = jnp.zeros_like(fcar)

    @pl.when(j > 0)
    def _():
        ubuf[0:CHUNK, :] = ubuf[tb:tb + CHUNK, :]

    ubuf[CHUNK:CHUNK + tb, 0:SSD_WIDTH] = xs_ref[...]
    ubuf[CHUNK:CHUNK + tb, SSD_WIDTH:SSD_WIDTH + BC_WIDTH] = b_ref[...]
    ubuf[CHUNK:CHUNK + tb, SSD_WIDTH + BC_WIDTH:CONV_DIM] = c_ref[...]

    slab = 2 * LANES

    def conv_silu(c, xcf, xsb):
        for s0 in range(0, CONV_DIM, slab):
            cols = slice(s0, s0 + slab)
            win = ubuf[c * CHUNK:(c + 2) * CHUNK, cols]
            acc = cb_ref[:, cols] + cw_ref[CONV_K - 1:CONV_K, cols] * win[CHUNK:2 * CHUNK, :].astype(F32)
            for sh in range(1, CONV_K):
                tap = CONV_K - 1 - sh
                acc = acc + cw_ref[tap:tap + 1, cols] * _dot(shift_ref[sh - 1], win)
            xc = _silu(acc)
            xcf[:, cols] = xc
            if s0 < SSD_WIDTH:
                xsb[:, cols] = xc.astype(BF16)

    lane = lax.broadcasted_iota(jnp.int32, (1, LANES), 1)
    is_f = lane < LANE_DT
    half_lane = lane < SSD_HEAD_DIM
    a_row = jnp.where((lane >= LANE_DT) & (lane < LANE_DT + SSD_HEADS), -jnp.exp(alog_ref[...]), 0.0)
    ri = lax.broadcasted_iota(jnp.int32, (CHUNK, CHUNK), 0)
    ci = lax.broadcasted_iota(jnp.int32, (CHUNK, CHUNK), 1)
    causal = ri >= ci
    tril = tril_ref[...]

    for c in range(tb // CHUNK):
        r0 = c * CHUNK
        xcf, xsb, ybuf, inter = xcf2.at[c % 2], xsb2.at[c % 2], ybuf2.at[c % 2], inter2.at[c % 2]
        conv_silu(c, xcf, xsb)

        pre = sm_ref[r0:r0 + CHUNK, :] + bias_ref[...]
        tail_term = jnp.log1p(jnp.exp(-jnp.abs(pre)))
        sp = jnp.maximum(pre, 0.0) + tail_term
        val = jnp.where(is_f, -(jnp.maximum(-pre, 0.0) + tail_term), sp * a_row)
        v_hi, v_mid, v_lo = _split3(val)
        cum = _dot(tril, v_hi) + _dot(tril, v_mid) + _dot(tril, v_lo)
        cum = cum + fcar[...]
        fcar[...] = jnp.where(is_f, cum[CHUNK - 1:CHUNK, :], 0.0)
        cum_t = cum.T
        dt_t = sp.T
        n_hi, n_mid, n_lo = _split3(jnp.where(is_f, cum * -LOG2E, 0.0))
        f_ref[r0:r0 + CHUNK, :] = (n_hi.astype(F32) + pltpu.roll(n_mid.astype(F32), FOX_HEADS, 1)
                                   + pltpu.roll(n_lo.astype(F32), 2 * FOX_HEADS, 1)).astype(BF16)
        e_col = jnp.exp(cum)

        for g in range(SSD_GROUPS):
            bcol = SSD_WIDTH + g * SSD_STATE
            bm = xcf[:, bcol:bcol + SSD_STATE]
            cm_b = xcf[:, bcol + BC_WIDTH:bcol + BC_WIDTH + SSD_STATE].astype(BF16)
            cb = _dot_nt(cm_b, bm.astype(BF16))
            bm_t = bm.T
            inter[...] = _dot(cm_b, st_ref[g].astype(BF16))
            for pr in range(SSD_HEADS_PER_GROUP // 2):
                gp = g * (SSD_HEADS_PER_GROUP // 2) + pr
                xp_b = xsb[:, gp * LANES:(gp + 1) * LANES]
                ys = []
                ups = []
                ecs = []
                els = []
                for hh in range(2):
                    ln = LANE_DT + 2 * gp + hh
                    a_c = cum[:, ln:ln + 1]
                    a_r = cum_t[ln:ln + 1, :]
                    d_r = dt_t[ln:ln + 1, :]
                    lm = jnp.where(causal, jnp.exp(a_c - a_r), 0.0)
                    m = cb * lm * d_r
                    ys.append(_dot(m.astype(BF16), xp_b))
                    last = a_r[:, CHUNK - 1:CHUNK]
                    w_r = d_r * jnp.exp(last - a_r)
                    ups.append(_dot((bm_t * w_r).astype(BF16), xp_b))
                    ecs.append(e_col[:, ln:ln + 1])
                    els.append(jnp.exp(last))
                lanes = slice(pr * LANES, (pr + 1) * LANES)
                y_pair = (jnp.where(half_lane, ys[0], ys[1])
                          + jnp.where(half_lane, ecs[0], ecs[1]) * inter[:, lanes])
                ybuf[:, gp * LANES:(gp + 1) * LANES] = y_pair
                st_ref[g, :, lanes] = (st_ref[g, :, lanes] * jnp.where(half_lane, els[0], els[1])
                                       + jnp.where(half_lane, ups[0], ups[1]))

        ss = jnp.zeros((CHUNK, 1), F32)
        for s0 in range(0, SSD_WIDTH, slab):
            cols = slice(s0, s0 + slab)
            gated = ((ybuf[:, cols] + dskip_ref[:, cols] * xcf[:, cols])
                     * _silu(z_ref[r0:r0 + CHUNK, cols].astype(F32)))
            ybuf[:, cols] = gated
            ss = ss + jnp.sum(gated * gated, axis=-1, keepdims=True)
        inv = lax.rsqrt(ss * (1.0 / SSD_WIDTH) + EPS)
        for s0 in range(0, SSD_WIDTH, slab):
            cols = slice(s0, s0 + slab)
            y_ref[r0:r0 + CHUNK, cols] = (ybuf[:, cols] * inv * ng_ref[:, cols]).astype(y_ref.dtype)


def _shift_matrices():
    m = np.zeros((CONV_K - 1, CHUNK, 2 * CHUNK), np.float32)
    t = np.arange(CHUNK)
    for sh in range(1, CONV_K):
        m[sh - 1, t, CHUNK + t - sh] = 1.0
    return jnp.asarray(m, BF16)


def _ssd(proj, small, conv_w, conv_b, bias_row, alog_row, dskip_row, norm_g, tril, shifts, bsz, seq):
    t = proj.shape[0]
    tb = min(TB_SSD, seq)
    per_b = seq // tb
    row = lambda b, j: b * per_b + j
    full = lambda shape: pl.BlockSpec(shape, lambda b, j: (0,) * len(shape))
    return pl.pallas_call(
        functools.partial(_ssd_kernel, tb=tb),
        grid=(bsz, per_b),
        in_specs=[pl.BlockSpec((tb, SSD_WIDTH), lambda b, j: (row(b, j), COL_XS // SSD_WIDTH)),
                  pl.BlockSpec((tb, SSD_WIDTH), lambda b, j: (row(b, j), COL_Z // SSD_WIDTH)),
                  pl.BlockSpec((tb, BC_WIDTH), lambda b, j: (row(b, j), COL_B // BC_WIDTH)),
                  pl.BlockSpec((tb, BC_WIDTH), lambda b, j: (row(b, j), COL_C // BC_WIDTH)),
                  pl.BlockSpec((tb, SMALL_W), lambda b, j: (row(b, j), 0)),
                  full((CONV_K, CONV_DIM)), full((1, CONV_DIM)), full((1, SMALL_W)),
                  full((1, SMALL_W)), full((1, SSD_WIDTH)), full((1, SSD_WIDTH)),
                  full((CHUNK, CHUNK)), full((CONV_K - 1, CHUNK, 2 * CHUNK))],
        out_specs=[pl.BlockSpec((tb, SSD_WIDTH), lambda b, j: (row(b, j), 0)),
                   pl.BlockSpec((tb, SMALL_W), lambda b, j: (row(b, j), 0))],
        out_shape=[jax.ShapeDtypeStruct((t, SSD_WIDTH), BF16),
                   jax.ShapeDtypeStruct((t, SMALL_W), BF16)],
        scratch_shapes=[pltpu.VMEM((CHUNK + tb, CONV_DIM), BF16),
                        pltpu.VMEM((SSD_GROUPS, SSD_STATE, SSD_HEADS_PER_GROUP * SSD_HEAD_DIM), F32),
                        pltpu.VMEM((2, CHUNK, SSD_WIDTH), F32),
                        pltpu.VMEM((1, SMALL_W), F32),
                        pltpu.VMEM((2, CHUNK, CONV_DIM), F32),
                        pltpu.VMEM((2, CHUNK, SSD_WIDTH), BF16),
                        pltpu.VMEM((2, CHUNK, SSD_HEADS_PER_GROUP * SSD_HEAD_DIM), F32)],
        compiler_params=_cparams(("parallel", "arbitrary")),
        name="ssd",
    )(proj, proj, proj, proj, small, conv_w, conv_b, bias_row, alog_row, dskip_row, norm_g, tril,
      shifts)


def _fox_bounds_kernel(q_ref, k_ref, nf_ref, nft_ref, o_ref, *, tq, nq):
    pair = pl.program_id(1)
    rows = 16
    hrow = lax.broadcasted_iota(jnp.int32, (rows, LANES), 0)
    hlane = lax.broadcasted_iota(jnp.int32, (rows, LANES), 1)
    selector = jnp.where((hrow < 2) & ((hlane >= FOX_HEAD_DIM) == (hrow == 1)), 1.0, 0.0).astype(BF16)
    q = q_ref[...]
    k = k_ref[...]
    qn2 = _dot_nt(selector, q * q)
    kn2 = jnp.max(_dot_nt(selector, k * k), axis=1, keepdims=True)
    qmax2 = jnp.zeros((rows, LANES), F32)
    for t in range(nq):
        tile_max = jnp.max(qn2[:, t * tq:(t + 1) * tq], axis=1, keepdims=True)
        qmax2 = jnp.where(hlane == t, tile_max, qmax2)
    qk = jnp.sqrt(qmax2 * kn2) * NORM_SLACK

    nft = nft_ref[0]
    nft_prev = pltpu.roll(nft, 1, 1)
    nf = nf_ref[0]
    sub8 = lax.broadcasted_iota(jnp.int32, (FOX_HEADS, LANES), 0)
    lane8 = lax.broadcasted_iota(jnp.int32, (nq, FOX_HEADS), 1)
    kb_idx = lax.broadcasted_iota(jnp.int32, (nq, LANES), 0)
    t_idx = lax.broadcasted_iota(jnp.int32, (nq, LANES), 1)
    first = None
    for h in range(2):
        hd = 2 * pair + h
        prev_row = jnp.sum(jnp.where(sub8 == hd, nft_prev, 0.0), axis=0, keepdims=True)
        nf_col = jnp.sum(jnp.where(lane8 == hd, nf, 0.0), axis=1, keepdims=True)
        thresh = prev_row - 2.0 * qk[h:h + 1, :] - PRUNE_MARGIN
        dead = (nf_col <= thresh) & (kb_idx < t_idx)
        count = jnp.sum(jnp.where(dead, 1.0, 0.0), axis=0, keepdims=True)
        first = count if first is None else jnp.minimum(first, count)
    o_ref[0, 0] = jnp.broadcast_to(first, (FOX_HEADS, LANES)).astype(jnp.int32)


def _fox_bounds(proj, f_aug, bsz, seq, tq):
    nq = seq // tq
    base = COL_FOX // LANES
    nblk = FOX_WIDTH // LANES
    ends = f_aug.reshape(bsz, nq, tq, SMALL_W)[:, :, tq - 1, 0:3 * FOX_HEADS].astype(F32)
    nf = ends.reshape(bsz, nq, 3, FOX_HEADS).sum(axis=2)
    nft = jnp.full((bsz, FOX_HEADS, LANES), jnp.inf, F32).at[:, :, 0:nq].set(nf.transpose(0, 2, 1))
    out = pl.pallas_call(
        functools.partial(_fox_bounds_kernel, tq=tq, nq=nq),
        grid=(bsz, FOX_PAIRS),
        in_specs=[pl.BlockSpec((seq, LANES), lambda b, p: (b, base + p)),
                  pl.BlockSpec((seq, LANES), lambda b, p: (b, base + nblk + p)),
                  pl.BlockSpec((1, nq, FOX_HEADS), lambda b, p: (b, 0, 0)),
                  pl.BlockSpec((1, FOX_HEADS, LANES), lambda b, p: (b, 0, 0))],
        out_specs=pl.BlockSpec((1, 1, FOX_HEADS, LANES), lambda b, p: (b, p, 0, 0)),
        out_shape=jax.ShapeDtypeStruct((bsz, FOX_PAIRS, FOX_HEADS, LANES), jnp.int32),
        compiler_params=_cparams(("parallel", "parallel")),
        name="fox_bounds",
    )(proj, proj, nf, nft)
    return out[:, :, 0, 0:nq].reshape(-1)


def _fox_kernel(start_ref, q_ref, k_ref, v_ref, g_ref, f_ref, o_ref, kaug, vaug, m_ref, acc_ref,
                *, tq, nq):
    bat = pl.program_id(0)
    pair = pl.program_id(1)
    i = pl.program_id(2)
    lane = lax.broadcasted_iota(jnp.int32, (1, LANES), 1)
    head0 = lane < FOX_HEAD_DIM

    @pl.when(i == 0)
    def _():
        kaug[:, 0:LANES] = k_ref[...]
        kaug[:, LANES:2 * LANES] = f_ref[...]
        vaug[:, 0:LANES] = v_ref[...]
        vaug[:, LANES:2 * LANES] = jnp.ones((vaug.shape[0], LANES), BF16)

    q = q_ref[...]
    zero = jnp.zeros_like(q)
    qh = []
    for h in range(2):
        hd = 2 * pair + h
        sel = (lane == hd) | (lane == hd + FOX_HEADS) | (lane == hd + 2 * FOX_HEADS)
        ones = jnp.broadcast_to(jnp.where(sel, 1.0, 0.0).astype(BF16), (tq, LANES))
        qm = jnp.where(head0, q, zero) if h == 0 else jnp.where(head0, zero, q)
        qh.append(jnp.concatenate([qm, ones], axis=1))
    start = start_ref[(bat * FOX_PAIRS + pair) * nq + i]

    m_ref[...] = jnp.full(m_ref.shape, NEG_BIG, F32)
    acc_ref[...] = jnp.zeros_like(acc_ref)

    def block(kb, masked):
        cols = pl.ds(pl.multiple_of(kb * tq, tq), tq)
        k = kaug[cols, :]
        v = vaug[cols, :]
        if masked:
            ri = lax.broadcasted_iota(jnp.int32, (tq, tq), 0)
            ci = lax.broadcasted_iota(jnp.int32, (tq, tq), 1)
            keep = ri >= ci
        for h in range(2):
            s = _dot_nt(qh[h], k)
            if masked:
                s = jnp.where(keep, s, NEG_BIG)
            m_old = m_ref[h]
            m_new = jnp.maximum(m_old, jnp.max(s, axis=-1, keepdims=True))
            alpha = jnp.exp2(m_old - m_new)
            p = jnp.exp2(s - jnp.concatenate([m_new] * (tq // LANES), axis=1))
            acc_ref[h] = (jnp.concatenate([alpha, alpha], axis=1) * acc_ref[h]
                          + _dot(p.astype(BF16), v))
            m_ref[h] = m_new

    def body(kq, carry):
        for u in range(KV_UNROLL):
            block(start + KV_UNROLL * kq + u, False)
        return carry

    full = (i - start) // KV_UNROLL
    lax.fori_loop(0, full, body, 0)
    tail0 = start + full * KV_UNROLL
    for extra in range(KV_UNROLL):
        @pl.when(i - tail0 == extra)
        def _(extra=extra):
            for u in range(extra):
                block(tail0 + u, False)
            block(i, True)

    o = jnp.where(head0, acc_ref[0, :, 0:LANES] / acc_ref[0, :, LANES:2 * LANES],
                  acc_ref[1, :, 0:LANES] / acc_ref[1, :, LANES:2 * LANES])
    o = o * _silu(g_ref[...].astype(F32))
    o_ref[...] = o.astype(o_ref.dtype)


def _fox(proj, f_aug, bsz, seq):
    t = proj.shape[0]
    tq = min(T_ATT, seq)
    nq = seq // tq
    base = COL_FOX // LANES
    nblk = FOX_WIDTH // LANES
    starts = _fox_bounds(proj, f_aug, bsz, seq, tq)
    return pl.pallas_call(
        functools.partial(_fox_kernel, tq=tq, nq=nq),
        grid=(bsz, FOX_PAIRS, nq),
        in_specs=[pl.BlockSpec(memory_space=pltpu.SMEM),
                  pl.BlockSpec((tq, LANES), lambda b, p, i: (b * nq + i, base + p)),
                  pl.BlockSpec((seq, LANES), lambda b, p, i: (b, base + nblk + p)),
                  pl.BlockSpec((seq, LANES), lambda b, p, i: (b, base + 2 * nblk + p)),
                  pl.BlockSpec((tq, LANES), lambda b, p, i: (b * nq + i, base + 3 * nblk + p)),
                  pl.BlockSpec((seq, SMALL_W), lambda b, p, i: (b, 0))],
        out_specs=pl.BlockSpec((tq, LANES), lambda b, p, i: (b * nq + i, p)),
        out_shape=jax.ShapeDtypeStruct((t, FOX_WIDTH), BF16),
        scratch_shapes=[pltpu.VMEM((seq, 2 * LANES), BF16),
                        pltpu.VMEM((seq, 2 * LANES), BF16),
                        pltpu.VMEM((2, tq, LANES), F32),
                        pltpu.VMEM((2, tq, 2 * LANES), F32)],
        compiler_params=_cparams(("parallel", "parallel", "arbitrary")),
        name="fox_attention",
    )(starts, proj, proj, proj, proj, f_aug)


def _out_kernel(ret_ref, ssd_ref, fox_ref, w_ref, x_ref, gate_ref, g_ref, sc_ref, sh_ref,
                *out_refs, emit_x):
    acc = _dot(ret_ref[...], w_ref[0:RET_WIDTH, :])
    acc = acc + _dot(ssd_ref[...], w_ref[RET_WIDTH:RET_WIDTH + SSD_WIDTH, :])
    acc = acc + _dot(fox_ref[...], w_ref[RET_WIDTH + SSD_WIDTH:MIX_WIDTH, :])
    xn = x_ref[...] + gate_ref[0] * acc
    if emit_x:
        out_refs[0][...] = xn
    y_ref = out_refs[-1]
    y_ref[...] = _modnorm(xn, g_ref[...], sc_ref[0], sh_ref[0]).astype(y_ref.dtype)


def _out_projection(ret, ssd, fox, w_out, x2, gate, g, scale, shift, seq, y_dtype, emit_x):
    t, d = x2.shape
    tm = min(TM_OUT, seq)
    per_b = seq // tm
    rowblk = lambda w: pl.BlockSpec((tm, w), lambda i: (i, 0))
    perb = pl.BlockSpec((1, 1, d), lambda i: (i // per_b, 0, 0))
    out_specs = [rowblk(d)]
    out_shape = [jax.ShapeDtypeStruct((t, d), y_dtype)]
    if emit_x:
        out_specs = [rowblk(d)] + out_specs
        out_shape = [jax.ShapeDtypeStruct((t, d), F32)] + out_shape
    return pl.pallas_call(
        functools.partial(_out_kernel, emit_x=emit_x),
        grid=(t // tm,),
        in_specs=[rowblk(RET_WIDTH), rowblk(SSD_WIDTH), rowblk(FOX_WIDTH),
                  pl.BlockSpec((MIX_WIDTH, d), lambda i: (0, 0)),
                  rowblk(d), perb, pl.BlockSpec((1, d), lambda i: (0, 0)), perb, perb],
        out_specs=out_specs,
        out_shape=out_shape,
        compiler_params=_cparams(("parallel",)),
        name="out_proj",
    )(ret, ssd, fox, w_out, x2, gate, g.reshape(1, d), scale, shift)


def _permute_in_weights(w_in):
    o_xbc = 4 * RET_WIDTH
    o_dt = o_xbc + CONV_DIM
    o_z = o_dt + SSD_HEADS
    o_fox = o_z + SSD_WIDTH
    o_f = o_fox + 4 * FOX_WIDTH
    main = jnp.concatenate([
        w_in[..., 0:o_xbc],
        w_in[..., o_xbc:o_xbc + SSD_WIDTH],
        w_in[..., o_z:o_z + SSD_WIDTH],
        w_in[..., o_xbc + SSD_WIDTH:o_dt],
        w_in[..., o_fox:o_fox + FOX_WIDTH] * FOX_Q_SCALE,
        w_in[..., o_fox + FOX_WIDTH:o_f],
    ], axis=-1).astype(BF16)
    pad = jnp.zeros(w_in.shape[:-1] + (SMALL_W - FOX_HEADS - SSD_HEADS,), w_in.dtype)
    small = jnp.concatenate([w_in[..., o_f:o_f + FOX_HEADS], w_in[..., o_dt:o_z], pad],
                            axis=-1).astype(BF16)
    return main, small


def _lane_row(depth, pieces):
    row = jnp.zeros((depth, 1, SMALL_W), F32)
    for start, arr in pieces:
        row = row.at[:, 0, start:start + arr.shape[-1]].set(arr.astype(F32))
    return row


def kernel(x, c, positions, norm_g, w_ada, b_ada, w_in, conv_w, conv_b, dt_bias, a_log, d_skip,
           ssd_norm_g, b_forget, w_out, final_g):
    bsz, seq, d = x.shape
    depth = w_in.shape[0]
    t = bsz * seq

    mod = _modulation(c, w_ada, b_ada)
    shift = mod[:, :, 0:d].reshape(depth, bsz, 1, d)
    scale = mod[:, :, d:2 * d].reshape(depth, bsz, 1, d)
    gate = mod[:, :, 2 * d:3 * d].reshape(depth, bsz, 1, d)

    cos_t, sin_t = _rope_tables(positions)
    w_main, w_small = _permute_in_weights(w_in)
    w_out_b = w_out.astype(BF16)
    ret_consts = _retention_consts()
    tril = jnp.asarray(np.tril(np.ones((CHUNK, CHUNK), np.float32)), BF16)
    shifts = _shift_matrices()
    bias_rows = _lane_row(depth, [(LANE_F, b_forget), (LANE_DT, dt_bias)])
    alog_rows = _lane_row(depth, [(LANE_DT, a_log)])
    dskip_rows = jnp.repeat(d_skip, SSD_HEAD_DIM, axis=-1).reshape(depth, 1, SSD_WIDTH)
    zeros_bd = jnp.zeros((bsz, 1, d), F32)

    x2 = x.reshape(t, d)
    h = _first_norm(x2, norm_g[0], scale[0], shift[0], seq)
    out = None
    for l in range(depth):
        proj, small = _in_projection(h, w_main[l], w_small[l])
        ret = _retention(proj, cos_t, sin_t, ret_consts, bsz, seq)
        ssd, f_rows = _ssd(proj, small, conv_w[l], conv_b[l].reshape(1, CONV_DIM), bias_rows[l],
                           alog_rows[l], dskip_rows[l], ssd_norm_g[l].reshape(1, SSD_WIDTH), tril,
                           shifts, bsz, seq)
        fox = _fox(proj, f_rows, bsz, seq)
        if l + 1 < depth:
            x2, h = _out_projection(ret, ssd, fox, w_out_b[l], x2, gate[l], norm_g[l + 1],
                                    scale[l + 1], shift[l + 1], seq, BF16, True)
        else:
            (out,) = _out_projection(ret, ssd, fox, w_out_b[l], x2, gate[l], final_g,
                                     zeros_bd, zeros_bd, seq, F32, False)
    return out.reshape(bsz, seq, d)
```

```python
import functools
import math

import numpy as np
import jax
import jax.numpy as jnp
from jax import lax
from jax.experimental import pallas as pl
from jax.experimental.pallas import tpu as pltpu

F32 = jnp.float32
BF16 = jnp.bfloat16

RET_HEADS = 4
RET_HEAD_DIM = 128
RET_WIDTH = RET_HEADS * RET_HEAD_DIM
SSD_HEADS = 16
SSD_HEAD_DIM = 64
SSD_WIDTH = SSD_HEADS * SSD_HEAD_DIM
SSD_GROUPS = 2
SSD_STATE = 128
SSD_HEADS_PER_GROUP = SSD_HEADS // SSD_GROUPS
CONV_K = 4
BC_WIDTH = SSD_GROUPS * SSD_STATE
CONV_DIM = SSD_WIDTH + 2 * BC_WIDTH
FOX_HEADS = 8
FOX_HEAD_DIM = 64
FOX_WIDTH = FOX_HEADS * FOX_HEAD_DIM
FOX_PAIRS = FOX_HEADS // 2
MIX_WIDTH = RET_WIDTH + SSD_WIDTH + FOX_WIDTH
CHUNK = 128
ROPE_BASE = 10000.0
EPS = 1e-6
LOG2E = math.log2(math.e)
FOX_Q_SCALE = FOX_HEAD_DIM ** -0.5 * LOG2E

COL_RET = 0
COL_XS = 4 * RET_WIDTH
COL_Z = COL_XS + SSD_WIDTH
COL_B = COL_Z + SSD_WIDTH
COL_C = COL_B + BC_WIDTH
COL_FOX = COL_C + BC_WIDTH
N_MAIN = COL_FOX + 4 * FOX_WIDTH
SMALL_W = 128
LANE_F = 0
LANE_DT = FOX_HEADS

LANES = 128
TM_PROJ = 2048
TN_PROJ = 1664
TM_OUT = 512
TM_NORM = 1024
TB_RET = 1024
TB_SSD = 512
T_ATT = 512
KV_UNROLL = 4
PRUNE_MARGIN = 152.0
NORM_SLACK = 1.02
VMEM_LIMIT = 48 * 1024 * 1024
NEG_BIG = -1e30


def _cparams(sem):
    return pltpu.CompilerParams(dimension_semantics=sem, vmem_limit_bytes=VMEM_LIMIT)


def _dot(a, b):
    return jnp.dot(a, b, preferred_element_type=F32)


def _dot_nt(a, b):
    return lax.dot_general(a, b, (((1,), (1,)), ((), ())), preferred_element_type=F32)


def _dot_tn(a, b):
    return lax.dot_general(a, b, (((0,), (0,)), ((), ())), preferred_element_type=F32)


def _silu(t):
    return t * jax.nn.sigmoid(t)


def _split3(t):
    hi = t.astype(BF16)
    r1 = t - hi.astype(F32)
    mid = r1.astype(BF16)
    lo = (r1 - mid.astype(F32)).astype(BF16)
    return hi, mid, lo


def _modnorm(t, g, scale, shift):
    var = jnp.mean(t * t, axis=-1, keepdims=True)
    return t * lax.rsqrt(var + EPS) * g * (1.0 + scale) + shift


def _mod_kernel(c_ref, w_ref, b_ref, o_ref):
    a = _silu(c_ref[...])
    w = w_ref[0]
    a_hi = a.astype(BF16)
    a_lo = (a - a_hi.astype(F32)).astype(BF16)
    w_hi = w.astype(BF16)
    w_lo = (w - w_hi.astype(F32)).astype(BF16)
    o_ref[0] = _dot(a_hi, w_hi) + _dot(a_hi, w_lo) + _dot(a_lo, w_hi) + b_ref[0]


def _modulation(c, w_ada, b_ada):
    depth, d, n3 = w_ada.shape
    bsz = c.shape[0]
    rows = 16
    c_pad = jnp.zeros((rows, d), F32).at[:bsz].set(c)
    tn = 1024
    out = pl.pallas_call(
        _mod_kernel,
        grid=(depth, n3 // tn),
        in_specs=[pl.BlockSpec((rows, d), lambda l, j: (0, 0)),
                  pl.BlockSpec((1, d, tn), lambda l, j: (l, 0, j)),
                  pl.BlockSpec((1, 1, tn), lambda l, j: (l, 0, j))],
        out_specs=pl.BlockSpec((1, rows, tn), lambda l, j: (l, 0, j)),
        out_shape=jax.ShapeDtypeStruct((depth, rows, n3), F32),
        compiler_params=_cparams(("parallel", "parallel")),
        name="adaln_mod",
    )(c_pad, w_ada, b_ada.reshape(depth, 1, n3))
    return out[:, :bsz]


def _rope_kernel(pos_ref, freq_ref, cos_ref, sin_ref):
    pos = pos_ref[0].astype(F32)
    ang = freq_ref[...] * pos
    c = jnp.cos(ang)
    s = jnp.sin(ang)
    cos_ref[...] = jnp.concatenate([c, c], axis=0).T
    sin_ref[...] = jnp.concatenate([-s, s], axis=0).T


def _rope_tables(positions):
    bsz, seq = positions.shape
    half = RET_HEAD_DIM // 2
    freq = (ROPE_BASE ** (-jnp.arange(half, dtype=F32) / half)).reshape(half, 1)
    tm = 512
    nt = seq // tm
    shp = jax.ShapeDtypeStruct((bsz * seq, RET_HEAD_DIM), F32)
    return pl.pallas_call(
        _rope_kernel,
        grid=(bsz, nt),
        in_specs=[pl.BlockSpec((1, 1, tm), lambda b, i: (b, 0, i)),
                  pl.BlockSpec((half, 1), lambda b, i: (0, 0))],
        out_specs=[pl.BlockSpec((tm, RET_HEAD_DIM), lambda b, i: (b * nt + i, 0)),
                   pl.BlockSpec((tm, RET_HEAD_DIM), lambda b, i: (b * nt + i, 0))],
        out_shape=[shp, shp],
        compiler_params=_cparams(("parallel", "parallel")),
        name="rope_tables",
    )(positions.reshape(bsz, 1, seq), freq)


def _norm_kernel(x_ref, g_ref, sc_ref, sh_ref, h_ref):
    h_ref[...] = _modnorm(x_ref[...], g_ref[...], sc_ref[0], sh_ref[0]).astype(h_ref.dtype)


def _first_norm(x2, g, scale, shift, seq):
    t, d = x2.shape
    tm = min(TM_NORM, seq)
    per_b = seq // tm
    return pl.pallas_call(
        _norm_kernel,
        grid=(t // tm,),
        in_specs=[pl.BlockSpec((tm, d), lambda i: (i, 0)),
                  pl.BlockSpec((1, d), lambda i: (0, 0)),
                  pl.BlockSpec((1, 1, d), lambda i: (i // per_b, 0, 0)),
                  pl.BlockSpec((1, 1, d), lambda i: (i // per_b, 0, 0))],
        out_specs=pl.BlockSpec((tm, d), lambda i: (i, 0)),
        out_shape=jax.ShapeDtypeStruct((t, d), BF16),
        compiler_params=_cparams(("parallel",)),
        name="first_norm",
    )(x2, g.reshape(1, d), scale, shift)


def _inproj_kernel(h_ref, w_ref, ws_ref, o_ref, os_ref):
    h = h_ref[...]
    o_ref[...] = _dot(h, w_ref[...]).astype(o_ref.dtype)

    @pl.when(pl.program_id(1) == 0)
    def _():
        os_ref[...] = _dot(h, ws_ref[...])


def _in_projection(h, w_main, w_small):
    t, d = h.shape
    tm = min(TM_PROJ, t)
    return pl.pallas_call(
        _inproj_kernel,
        grid=(t // tm, N_MAIN // TN_PROJ),
        in_specs=[pl.BlockSpec((tm, d), lambda i, j: (i, 0)),
                  pl.BlockSpec((d, TN_PROJ), lambda i, j: (0, j)),
                  pl.BlockSpec((d, SMALL_W), lambda i, j: (0, 0))],
        out_specs=[pl.BlockSpec((tm, TN_PROJ), lambda i, j: (i, j)),
                   pl.BlockSpec((tm, SMALL_W), lambda i, j: (i, 0))],
        out_shape=[jax.ShapeDtypeStruct((t, N_MAIN), BF16),
                   jax.ShapeDtypeStruct((t, SMALL_W), F32)],
        compiler_params=_cparams(("parallel", "arbitrary")),
        name="in_proj",
    )(h, w_main, w_small)


def _retention_consts():
    h = np.arange(RET_HEADS, dtype=np.float64)
    log_g = np.log(1.0 - 2.0 ** (-5.0 - h))
    idx = np.arange(CHUNK, dtype=np.float64)
    diff = idx[:, None] - idx[None, :]
    intra = np.where(diff >= 0, np.exp(log_g[:, None, None] * np.maximum(diff, 0.0)), 0.0)
    dq = np.exp(log_g[:, None] * (idx + 1.0))
    dk = np.exp(log_g[:, None] * (CHUNK - 1.0 - idx))
    dq = np.broadcast_to(dq[:, :, None], (RET_HEADS, CHUNK, RET_HEAD_DIM))
    dk = np.broadcast_to(dk[:, :, None], (RET_HEADS, CHUNK, RET_HEAD_DIM))
    dchunk = tuple(float(v) for v in np.exp(log_g * CHUNK))
    return (jnp.asarray(intra, F32), jnp.asarray(dq, F32), jnp.asarray(dk, F32), dchunk)


def _ret_kernel(q_ref, k_ref, v_ref, g_ref, cos_ref, sin_ref, di_ref, dq_ref, dk_ref,
                o_ref, s_ref, *, nchunk, dchunk):
    @pl.when(pl.program_id(1) == 0)
    def _():
        s_ref[...] = jnp.zeros_like(s_ref)

    kscale = RET_HEAD_DIM ** -0.5

    def body(c, carry):
        rows = pl.ds(pl.multiple_of(c * CHUNK, CHUNK), CHUNK)
        cs = cos_ref[rows, :]
        sn = sin_ref[rows, :]
        for h in range(RET_HEADS):
            cols = slice(h * RET_HEAD_DIM, (h + 1) * RET_HEAD_DIM)
            q = q_ref[rows, cols].astype(F32)
            k = k_ref[rows, cols].astype(F32)
            v = v_ref[rows, cols]
            qr = q * cs + pltpu.roll(q, RET_HEAD_DIM // 2, 1) * sn
            kr = (k * cs + pltpu.roll(k, RET_HEAD_DIM // 2, 1) * sn) * kscale
            qb = qr.astype(BF16)
            kb = kr.astype(BF16)
            state = s_ref[h]
            s = _dot_nt(qb, kb) * di_ref[h]
            o = _dot(s.astype(BF16), v) + _dot(qb, state.astype(BF16)) * dq_ref[h]
            kd = (kr * dk_ref[h]).astype(BF16)
            s_ref[h] = state * dchunk[h] + _dot_tn(kd, v)
            o = o * lax.rsqrt(jnp.mean(o * o, axis=-1, keepdims=True) + EPS)
            o = o * _silu(g_ref[rows, cols].astype(F32))
            o_ref[rows, cols] = o.astype(o_ref.dtype)
        return carry

    lax.fori_loop(0, nchunk, body, 0, unroll=4)


def _retention(proj, cos_t, sin_t, consts, bsz, seq):
    t = proj.shape[0]
    tb = min(TB_RET, seq)
    per_b = seq // tb
    intra, dq, dk, dchunk = consts
    row = lambda b, j: b * per_b + j
    cblk = lambda idx: pl.BlockSpec((tb, RET_WIDTH), lambda b, j: (row(b, j), idx))
    const3 = pl.BlockSpec((RET_HEADS, CHUNK, RET_HEAD_DIM), lambda b, j: (0, 0, 0))
    tblk = pl.BlockSpec((tb, RET_HEAD_DIM), lambda b, j: (row(b, j), 0))
    base = COL_RET // RET_WIDTH
    return pl.pallas_call(
        functools.partial(_ret_kernel, nchunk=tb // CHUNK, dchunk=dchunk),
        grid=(bsz, per_b),
        in_specs=[cblk(base), cblk(base + 1), cblk(base + 2), cblk(base + 3),
                  tblk, tblk, const3, const3, const3],
        out_specs=pl.BlockSpec((tb, RET_WIDTH), lambda b, j: (row(b, j), 0)),
        out_shape=jax.ShapeDtypeStruct((t, RET_WIDTH), BF16),
        scratch_shapes=[pltpu.VMEM((RET_HEADS, RET_HEAD_DIM, RET_HEAD_DIM), F32)],
        compiler_params=_cparams(("parallel", "arbitrary")),
        name="retention",
    )(proj, proj, proj, proj, cos_t, sin_t, intra, dq, dk)


def _ssd_kernel(xs_ref, z_ref, b_ref, c_ref, sm_ref, cw_ref, cb_ref, bias_ref, alog_ref,
                dskip_ref, ng_ref, tril_ref, shift_ref, y_ref, f_ref,
                ubuf, st_ref, ybuf2, fcar, xcf2, xsb2, inter2, *, tb):
    j = pl.program_id(1)

    @pl.when(j == 0)
    def _():
        ubuf[0:CHUNK, :] = jnp.zeros((CHUNK, CONV_DIM), BF16)
        st_ref[...] = jnp.zeros_like(st_ref)
        fcar[...] = jnp.zeros_like(fcar)

    @pl.when(j > 0)
    def _():
        ubuf[0:CHUNK, :] = ubuf[tb:tb + CHUNK, :]

    ubuf[CHUNK:CHUNK + tb, 0:SSD_WIDTH] = xs_ref[...]
    ubuf[CHUNK:CHUNK + tb, SSD_WIDTH:SSD_WIDTH + BC_WIDTH] = b_ref[...]
    ubuf[CHUNK:CHUNK + tb, SSD_WIDTH + BC_WIDTH:CONV_DIM] = c_ref[...]

    slab = 2 * LANES

    def conv_silu(c, xcf, xsb):
        for s0 in range(0, CONV_DIM, slab):
            cols = slice(s0, s0 + slab)
            win = ubuf[c * CHUNK:(c + 2) * CHUNK, cols]
            acc = cb_ref[:, cols] + cw_ref[CONV_K - 1:CONV_K, cols] * win[CHUNK:2 * CHUNK, :].astype(F32)
            for sh in range(1, CONV_K):
                tap = CONV_K - 1 - sh
                acc = acc + cw_ref[tap:tap + 1, cols] * _dot(shift_ref[sh - 1], win)
            xc = _silu(acc)
            xcf[:, cols] = xc
            if s0 < SSD_WIDTH:
                xsb[:, cols] = xc.astype(BF16)

    lane = lax.broadcasted_iota(jnp.int32, (1, LANES), 1)
    is_f = lane < LANE_DT
    half_lane = lane < SSD_HEAD_DIM
    a_row = jnp.where((lane >= LANE_DT) & (lane < LANE_DT + SSD_HEADS), -jnp.exp(alog_ref[...]), 0.0)
    ri = lax.broadcasted_iota(jnp.int32, (CHUNK, CHUNK), 0)
    ci = lax.broadcasted_iota(jnp.int32, (CHUNK, CHUNK), 1)
    causal = ri >= ci
    tril = tril_ref[...]

    for c in range(tb // CHUNK):
        r0 = c * CHUNK
        xcf, xsb, ybuf, inter = xcf2.at[c % 2], xsb2.at[c % 2], ybuf2.at[c % 2], inter2.at[c % 2]
        conv_silu(c, xcf, xsb)

        pre = sm_ref[r0:r0 + CHUNK, :] + bias_ref[...]
        tail_term = jnp.log1p(jnp.exp(-jnp.abs(pre)))
        sp = jnp.maximum(pre, 0.0) + tail_term
        val = jnp.where(is_f, -(jnp.maximum(-pre, 0.0) + tail_term), sp * a_row)
        v_hi, v_mid, v_lo = _split3(val)
        cum = _dot(tril, v_hi) + _dot(tril, v_mid) + _dot(tril, v_lo)
        cum = cum + fcar[...]
        fcar[...] = jnp.where(is_f, cum[CHUNK - 1:CHUNK, :], 0.0)
        cum_t = cum.T
        dt_t = sp.T
        n_hi, n_mid, n_lo = _split3(jnp.where(is_f, cum * -LOG2E, 0.0))
        f_ref[r0:r0 + CHUNK, :] = (n_hi.astype(F32) + pltpu.roll(n_mid.astype(F32), FOX_HEADS, 1)
                                   + pltpu.roll(n_lo.astype(F32), 2 * FOX_HEADS, 1)).astype(BF16)
        e_col = jnp.exp(cum)

        for g in range(SSD_GROUPS):
            bcol = SSD_WIDTH + g * SSD_STATE
            bm = xcf[:, bcol:bcol + SSD_STATE]
            cm_b = xcf[:, bcol + BC_WIDTH:bcol + BC_WIDTH + SSD_STATE].astype(BF16)
            cb = _dot_nt(cm_b, bm.astype(BF16))
            bm_t = bm.T
            inter[...] = _dot(cm_b, st_ref[g].astype(BF16))
            for pr in range(SSD_HEADS_PER_GROUP // 2):
                gp = g * (SSD_HEADS_PER_GROUP // 2) + pr
                xp_b = xsb[:, gp * LANES:(gp + 1) * LANES]
                ys = []
                ups = []
                ecs = []
                els = []
                for hh in range(2):
                    ln = LANE_DT + 2 * gp + hh
                    a_c = cum[:, ln:ln + 1]
                    a_r = cum_t[ln:ln + 1, :]
                    d_r = dt_t[ln:ln + 1, :]
                    lm = jnp.where(causal, jnp.exp(a_c - a_r), 0.0)
                    m = cb * lm * d_r
                    ys.append(_dot(m.astype(BF16), xp_b))
                    last = a_r[:, CHUNK - 1:CHUNK]
                    w_r = d_r * jnp.exp(last - a_r)
                    ups.append(_dot((bm_t * w_r).astype(BF16), xp_b))
                    ecs.append(e_col[:, ln:ln + 1])
                    els.append(jnp.exp(last))
                lanes = slice(pr * LANES, (pr + 1) * LANES)
                y_pair = (jnp.where(half_lane, ys[0], ys[1])
                          + jnp.where(half_lane, ecs[0], ecs[1]) * inter[:, lanes])
                ybuf[:, gp * LANES:(gp + 1) * LANES] = y_pair
                st_ref[g, :, lanes] = (st_ref[g, :, lanes] * jnp.where(half_lane, els[0], els[1])
                                       + jnp.where(half_lane, ups[0], ups[1]))

        ss = jnp.zeros((CHUNK, 1), F32)
        for s0 in range(0, SSD_WIDTH, slab):
            cols = slice(s0, s0 + slab)
            gated = ((ybuf[:, cols] + dskip_ref[:, cols] * xcf[:, cols])
                     * _silu(z_ref[r0:r0 + CHUNK, cols].astype(F32)))
            ybuf[:, cols] = gated
            ss = ss + jnp.sum(gated * gated, axis=-1, keepdims=True)
        inv = lax.rsqrt(ss * (1.0 / SSD_WIDTH) + EPS)
        for s0 in range(0, SSD_WIDTH, slab):
            cols = slice(s0, s0 + slab)
            y_ref[r0:r0 + CHUNK, cols] = (ybuf[:, cols] * inv * ng_ref[:, cols]).astype(y_ref.dtype)


def _shift_matrices():
    m = np.zeros((CONV_K - 1, CHUNK, 2 * CHUNK), np.float32)
    t = np.arange(CHUNK)
    for sh in range(1, CONV_K):
        m[sh - 1, t, CHUNK + t - sh] = 1.0
    return jnp.asarray(m, BF16)


def _ssd(proj, small, conv_w, conv_b, bias_row, alog_row, dskip_row, norm_g, tril, shifts, bsz, seq):
    t = proj.shape[0]
    tb = min(TB_SSD, seq)
    per_b = seq // tb
    row = lambda b, j: b * per_b + j
    full = lambda shape: pl.BlockSpec(shape, lambda b, j: (0,) * len(shape))
    return pl.pallas_call(
        functools.partial(_ssd_kernel, tb=tb),
        grid=(bsz, per_b),
        in_specs=[pl.BlockSpec((tb, SSD_WIDTH), lambda b, j: (row(b, j), COL_XS // SSD_WIDTH)),
                  pl.BlockSpec((tb, SSD_WIDTH), lambda b, j: (row(b, j), COL_Z // SSD_WIDTH)),
                  pl.BlockSpec((tb, BC_WIDTH), lambda b, j: (row(b, j), COL_B // BC_WIDTH)),
                  pl.BlockSpec((tb, BC_WIDTH), lambda b, j: (row(b, j), COL_C // BC_WIDTH)),
                  pl.BlockSpec((tb, SMALL_W), lambda b, j: (row(b, j), 0)),
                  full((CONV_K, CONV_DIM)), full((1, CONV_DIM)), full((1, SMALL_W)),
                  full((1, SMALL_W)), full((1, SSD_WIDTH)), full((1, SSD_WIDTH)),
                  full((CHUNK, CHUNK)), full((CONV_K - 1, CHUNK, 2 * CHUNK))],
        out_specs=[pl.BlockSpec((tb, SSD_WIDTH), lambda b, j: (row(b, j), 0)),
                   pl.BlockSpec((tb, SMALL_W), lambda b, j: (row(b, j), 0))],
        out_shape=[jax.ShapeDtypeStruct((t, SSD_WIDTH), BF16),
                   jax.ShapeDtypeStruct((t, SMALL_W), BF16)],
        scratch_shapes=[pltpu.VMEM((CHUNK + tb, CONV_DIM), BF16),
                        pltpu.VMEM((SSD_GROUPS, SSD_STATE, SSD_HEADS_PER_GROUP * SSD_HEAD_DIM), F32),
                        pltpu.VMEM((2, CHUNK, SSD_WIDTH), F32),
                        pltpu.VMEM((1, SMALL_W), F32),
                        pltpu.VMEM((2, CHUNK, CONV_DIM), F32),
                        pltpu.VMEM((2, CHUNK, SSD_WIDTH), BF16),
                        pltpu.VMEM((2, CHUNK, SSD_HEADS_PER_GROUP * SSD_HEAD_DIM), F32)],
        compiler_params=_cparams(("parallel", "arbitrary")),
        name="ssd",
    )(proj, proj, proj, proj, small, conv_w, conv_b, bias_row, alog_row, dskip_row, norm_g, tril,
      shifts)


def _fox_bounds_kernel(q_ref, k_ref, nf_ref, nft_ref, o_ref, *, tq, nq):
    pair = pl.program_id(1)
    rows = 16
    hrow = lax.broadcasted_iota(jnp.int32, (rows, LANES), 0)
    hlane = lax.broadcasted_iota(jnp.int32, (rows, LANES), 1)
    selector = jnp.where((hrow < 2) & ((hlane >= FOX_HEAD_DIM) == (hrow == 1)), 1.0, 0.0).astype(BF16)
    q = q_ref[...]
    k = k_ref[...]
    qn2 = _dot_nt(selector, q * q)
    kn2 = jnp.max(_dot_nt(selector, k * k), axis=1, keepdims=True)
    qmax2 = jnp.zeros((rows, LANES), F32)
    for t in range(nq):
        tile_max = jnp.max(qn2[:, t * tq:(t + 1) * tq], axis=1, keepdims=True)
        qmax2 = jnp.where(hlane == t, tile_max, qmax2)
    qk = jnp.sqrt(qmax2 * kn2) * NORM_SLACK

    nft = nft_ref[0]
    nft_prev = pltpu.roll(nft, 1, 1)
    nf = nf_ref[0]
    sub8 = lax.broadcasted_iota(jnp.int32, (FOX_HEADS, LANES), 0)
    lane8 = lax.broadcasted_iota(jnp.int32, (nq, FOX_HEADS), 1)
    kb_idx = lax.broadcasted_iota(jnp.int32, (nq, LANES), 0)
    t_idx = lax.broadcasted_iota(jnp.int32, (nq, LANES), 1)
    first = None
    for h in range(2):
        hd = 2 * pair + h
        prev_row = jnp.sum(jnp.where(sub8 == hd, nft_prev, 0.0), axis=0, keepdims=True)
        nf_col = jnp.sum(jnp.where(lane8 == hd, nf, 0.0), axis=1, keepdims=True)
        thresh = prev_row - 2.0 * qk[h:h + 1, :] - PRUNE_MARGIN
        dead = (nf_col <= thresh) & (kb_idx < t_idx)
        count = jnp.sum(jnp.where(dead, 1.0, 0.0), axis=0, keepdims=True)
        first = count if first is None else jnp.minimum(first, count)
    o_ref[0, 0] = jnp.broadcast_to(first, (FOX_HEADS, LANES)).astype(jnp.int32)


def _fox_bounds(proj, f_aug, bsz, seq, tq):
    nq = seq // tq
    base = COL_FOX // LANES
    nblk = FOX_WIDTH // LANES
    ends = f_aug.reshape(bsz, nq, tq, SMALL_W)[:, :, tq - 1, 0:3 * FOX_HEADS].astype(F32)
    nf = ends.reshape(bsz, nq, 3, FOX_HEADS).sum(axis=2)
    nft = jnp.full((bsz, FOX_HEADS, LANES), jnp.inf, F32).at[:, :, 0:nq].set(nf.transpose(0, 2, 1))
    out = pl.pallas_call(
        functools.partial(_fox_bounds_kernel, tq=tq, nq=nq),
        grid=(bsz, FOX_PAIRS),
        in_specs=[pl.BlockSpec((seq, LANES), lambda b, p: (b, base + p)),
                  pl.BlockSpec((seq, LANES), lambda b, p: (b, base + nblk + p)),
                  pl.BlockSpec((1, nq, FOX_HEADS), lambda b, p: (b, 0, 0)),
                  pl.BlockSpec((1, FOX_HEADS, LANES), lambda b, p: (b, 0, 0))],
        out_specs=pl.BlockSpec((1, 1, FOX_HEADS, LANES), lambda b, p: (b, p, 0, 0)),
        out_shape=jax.ShapeDtypeStruct((bsz, FOX_PAIRS, FOX_HEADS, LANES), jnp.int32),
        compiler_params=_cparams(("parallel", "parallel")),
        name="fox_bounds",
    )(proj, proj, nf, nft)
    return out[:, :, 0, 0:nq].reshape(-1)


def _fox_kernel(start_ref, q_ref, k_ref, v_ref, g_ref, f_ref, o_ref, kaug, vaug, m_ref, acc_ref,
                *, tq, nq):
    bat = pl.program_id(0)
    pair = pl.program_id(1)
    i = pl.program_id(2)
    lane = lax.broadcasted_iota(jnp.int32, (1, LANES), 1)
    head0 = lane < FOX_HEAD_DIM

    @pl.when(i == 0)
    def _():
        kaug[:, 0:LANES] = k_ref[...]
        kaug[:, LANES:2 * LANES] = f_ref[...]
        vaug[:, 0:LANES] = v_ref[...]
        vaug[:, LANES:2 * LANES] = jnp.ones((vaug.shape[0], LANES), BF16)

    q = q_ref[...]
    zero = jnp.zeros_like(q)
    qh = []
    for h in range(2):
        hd = 2 * pair + h
        sel = (lane == hd) | (lane == hd + FOX_HEADS) | (lane == hd + 2 * FOX_HEADS)
        ones = jnp.broadcast_to(jnp.where(sel, 1.0, 0.0).astype(BF16), (tq, LANES))
        qm = jnp.where(head0, q, zero) if h == 0 else jnp.where(head0, zero, q)
        qh.append(jnp.concatenate([qm, ones], axis=1))
    start = start_ref[(bat * FOX_PAIRS + pair) * nq + i]

    m_ref[...] = jnp.full(m_ref.shape, NEG_BIG, F32)
    acc_ref[...] = jnp.zeros_like(acc_ref)

    def block(kb, masked):
        cols = pl.ds(pl.multiple_of(kb * tq, tq), tq)
        k = kaug[cols, :]
        v = vaug[cols, :]
        if masked:
            ri = lax.broadcasted_iota(jnp.int32, (tq, tq), 0)
            ci = lax.broadcasted_iota(jnp.int32, (tq, tq), 1)
            keep = ri >= ci
        for h in range(2):
            s = _dot_nt(qh[h], k)
            if masked:
                s = jnp.where(keep, s, NEG_BIG)
            m_old = m_ref[h]
            m_new = jnp.maximum(m_old, jnp.max(s, axis=-1, keepdims=True))
            alpha = jnp.exp2(m_old - m_new)
            p = jnp.exp2(s - jnp.concatenate([m_new] * (tq // LANES), axis=1))
            acc_ref[h] = (jnp.concatenate([alpha, alpha], axis=1) * acc_ref[h]
                          + _dot(p.astype(BF16), v))
            m_ref[h] = m_new

    def body(kq, carry):
        for u in range(KV_UNROLL):
            block(start + KV_UNROLL * kq + u, False)
        return carry

    full = (i - start) // KV_UNROLL
    lax.fori_loop(0, full, body, 0)
    tail0 = start + full * KV_UNROLL
    for extra in range(KV_UNROLL):
        @pl.when(i - tail0 == extra)
        def _(extra=extra):
            for u in range(extra):
                block(tail0 + u, False)
            block(i, True)

    o = jnp.where(head0, acc_ref[0, :, 0:LANES] / acc_ref[0, :, LANES:2 * LANES],
                  acc_ref[1, :, 0:LANES] / acc_ref[1, :, LANES:2 * LANES])
    o = o * _silu(g_ref[...].astype(F32))
    o_ref[...] = o.astype(o_ref.dtype)


def _fox(proj, f_aug, bsz, seq):
    t = proj.shape[0]
    tq = min(T_ATT, seq)
    nq = seq // tq
    base = COL_FOX // LANES
    nblk = FOX_WIDTH // LANES
    starts = _fox_bounds(proj, f_aug, bsz, seq, tq)
    return pl.pallas_call(
        functools.partial(_fox_kernel, tq=tq, nq=nq),
        grid=(bsz, FOX_PAIRS, nq),
        in_specs=[pl.BlockSpec(memory_space=pltpu.SMEM),
                  pl.BlockSpec((tq, LANES), lambda b, p, i: (b * nq + i, base + p)),
                  pl.BlockSpec((seq, LANES), lambda b, p, i: (b, base + nblk + p)),
                  pl.BlockSpec((seq, LANES), lambda b, p, i: (b, base + 2 * nblk + p)),
                  pl.BlockSpec((tq, LANES), lambda b, p, i: (b * nq + i, base + 3 * nblk + p)),
                  pl.BlockSpec((seq, SMALL_W), lambda b, p, i: (b, 0))],
        out_specs=pl.BlockSpec((tq, LANES), lambda b, p, i: (b * nq + i, p)),
        out_shape=jax.ShapeDtypeStruct((t, FOX_WIDTH), BF16),
        scratch_shapes=[pltpu.VMEM((seq, 2 * LANES), BF16),
                        pltpu.VMEM((seq, 2 * LANES), BF16),
                        pltpu.VMEM((2, tq, LANES), F32),
                        pltpu.VMEM((2, tq, 2 * LANES), F32)],
        compiler_params=_cparams(("parallel", "parallel", "arbitrary")),
        name="fox_attention",
    )(starts, proj, proj, proj, proj, f_aug)


def _out_kernel(ret_ref, ssd_ref, fox_ref, w_ref, x_ref, gate_ref, g_ref, sc_ref, sh_ref,
                *out_refs, emit_x):
    acc = _dot(ret_ref[...], w_ref[0:RET_WIDTH, :])
    acc = acc + _dot(ssd_ref[...], w_ref[RET_WIDTH:RET_WIDTH + SSD_WIDTH, :])
    acc = acc + _dot(fox_ref[...], w_ref[RET_WIDTH + SSD_WIDTH:MIX_WIDTH, :])
    xn = x_ref[...] + gate_ref[0] * acc
    if emit_x:
        out_refs[0][...] = xn
    y_ref = out_refs[-1]
    y_ref[...] = _modnorm(xn, g_ref[...], sc_ref[0], sh_ref[0]).astype(y_ref.dtype)


def _out_projection(ret, ssd, fox, w_out, x2, gate, g, scale, shift, seq, y_dtype, emit_x):
    t, d = x2.shape
    tm = min(TM_OUT, seq)
    per_b = seq // tm
    rowblk = lambda w: pl.BlockSpec((tm, w), lambda i: (i, 0))
    perb = pl.BlockSpec((1, 1, d), lambda i: (i // per_b, 0, 0))
    out_specs = [rowblk(d)]
    out_shape = [jax.ShapeDtypeStruct((t, d), y_dtype)]
    if emit_x:
        out_specs = [rowblk(d)] + out_specs
        out_shape = [jax.ShapeDtypeStruct((t, d), F32)] + out_shape
    return pl.pallas_call(
        functools.partial(_out_kernel, emit_x=emit_x),
        grid=(t // tm,),
        in_specs=[rowblk(RET_WIDTH), rowblk(SSD_WIDTH), rowblk(FOX_WIDTH),
                  pl.BlockSpec((MIX_WIDTH, d), lambda i: (0, 0)),
                  rowblk(d), perb, pl.BlockSpec((1, d), lambda i: (0, 0)), perb, perb],
        out_specs=out_specs,
        out_shape=out_shape,
        compiler_params=_cparams(("parallel",)),
        name="out_proj",
    )(ret, ssd, fox, w_out, x2, gate, g.reshape(1, d), scale, shift)


def _permute_in_weights(w_in):
    o_xbc = 4 * RET_WIDTH
    o_dt = o_xbc + CONV_DIM
    o_z = o_dt + SSD_HEADS
    o_fox = o_z + SSD_WIDTH
    o_f = o_fox + 4 * FOX_WIDTH
    col_scale = np.ones((w_in.shape[-1],), np.float32)
    col_scale[o_fox:o_fox + FOX_WIDTH] = FOX_Q_SCALE
    wb = (w_in * col_scale).astype(BF16)
    main = jnp.concatenate([
        wb[..., 0:o_xbc + SSD_WIDTH],
        wb[..., o_z:o_z + SSD_WIDTH],
        wb[..., o_xbc + SSD_WIDTH:o_dt],
        wb[..., o_fox:o_f],
    ], axis=-1)
    pad = jnp.zeros(wb.shape[:-1] + (SMALL_W - FOX_HEADS - SSD_HEADS,), BF16)
    small = jnp.concatenate([wb[..., o_f:o_f + FOX_HEADS], wb[..., o_dt:o_z], pad], axis=-1)
    return main, small


def _lane_row(depth, pieces):
    row = jnp.zeros((depth, 1, SMALL_W), F32)
    for start, arr in pieces:
        row = row.at[:, 0, start:start + arr.shape[-1]].set(arr.astype(F32))
    return row


def kernel(x, c, positions, norm_g, w_ada, b_ada, w_in, conv_w, conv_b, dt_bias, a_log, d_skip,
           ssd_norm_g, b_forget, w_out, final_g):
    bsz, seq, d = x.shape
    depth = w_in.shape[0]
    t = bsz * seq

    mod = _modulation(c, w_ada, b_ada)
    shift = mod[:, :, 0:d].reshape(depth, bsz, 1, d)
    scale = mod[:, :, d:2 * d].reshape(depth, bsz, 1, d)
    gate = mod[:, :, 2 * d:3 * d].reshape(depth, bsz, 1, d)

    cos_t, sin_t = _rope_tables(positions)
    w_main, w_small = _permute_in_weights(w_in)
    w_out_b = w_out.astype(BF16)
    ret_consts = _retention_consts()
    tril = jnp.asarray(np.tril(np.ones((CHUNK, CHUNK), np.float32)), BF16)
    shifts = _shift_matrices()
    bias_rows = _lane_row(depth, [(LANE_F, b_forget), (LANE_DT, dt_bias)])
    alog_rows = _lane_row(depth, [(LANE_DT, a_log)])
    dskip_rows = jnp.repeat(d_skip, SSD_HEAD_DIM, axis=-1).reshape(depth, 1, SSD_WIDTH)
    zeros_bd = jnp.zeros((bsz, 1, d), F32)

    x2 = x.reshape(t, d)
    h = _first_norm(x2, norm_g[0], scale[0], shift[0], seq)
    out = None
    for l in range(depth):
        proj, small = _in_projection(h, w_main[l], w_small[l])
        ret = _retention(proj, cos_t, sin_t, ret_consts, bsz, seq)
        ssd, f_rows = _ssd(proj, small, conv_w[l], conv_b[l].reshape(1, CONV_DIM), bias_rows[l],
                           alog_rows[l], dskip_rows[l], ssd_norm_g[l].reshape(1, SSD_WIDTH), tril,
                           shifts, bsz, seq)
        fox = _fox(proj, f_rows, bsz, seq)
        if l + 1 < depth:
            x2, h = _out_projection(ret, ssd, fox, w_out_b[l], x2, gate[l], norm_g[l + 1],
                                    scale[l + 1], shift[l + 1], seq, BF16, True)
        else:
            (out,) = _out_projection(ret, ssd, fox, w_out_b[l], x2, gate[l], final_g,
                                     zeros_bd, zeros_bd, seq, F32, False)
    return out.reshape(bsz, seq, d)
```

```python
import functools
import math

import numpy as np
import jax
import jax.numpy as jnp
from jax import lax
from jax.experimental import pallas as pl
from jax.experimental.pallas import tpu as pltpu

F32 = jnp.float32
BF16 = jnp.bfloat16

RET_HEADS = 4
RET_HEAD_DIM = 128
RET_WIDTH = RET_HEADS * RET_HEAD_DIM
SSD_HEADS = 16
SSD_HEAD_DIM = 64
SSD_WIDTH = SSD_HEADS * SSD_HEAD_DIM
SSD_GROUPS = 2
SSD_STATE = 128
SSD_HEADS_PER_GROUP = SSD_HEADS // SSD_GROUPS
CONV_K = 4
BC_WIDTH = SSD_GROUPS * SSD_STATE
CONV_DIM = SSD_WIDTH + 2 * BC_WIDTH
FOX_HEADS = 8
FOX_HEAD_DIM = 64
FOX_WIDTH = FOX_HEADS * FOX_HEAD_DIM
FOX_PAIRS = FOX_HEADS // 2
MIX_WIDTH = RET_WIDTH + SSD_WIDTH + FOX_WIDTH
CHUNK = 128
ROPE_BASE = 10000.0
EPS = 1e-6
LOG2E = math.log2(math.e)
FOX_Q_SCALE = FOX_HEAD_DIM ** -0.5 * LOG2E

COL_RET = 0
COL_XS = 4 * RET_WIDTH
COL_Z = COL_XS + SSD_WIDTH
COL_B = COL_Z + SSD_WIDTH
COL_C = COL_B + BC_WIDTH
COL_FOX = COL_C + BC_WIDTH
N_MAIN = COL_FOX + 4 * FOX_WIDTH
SMALL_W = 128
LANE_F = 0
LANE_DT = FOX_HEADS

LANES = 128
TM_PROJ = 2048
TN_PROJ = 1664
TM_OUT = 512
TM_NORM = 1024
TB_RET = 1024
TB_SSD = 1024
T_ATT = 512
KV_UNROLL = 4
PRUNE_MARGIN = 152.0
NORM_SLACK = 1.02
VMEM_LIMIT = 48 * 1024 * 1024
NEG_BIG = -1e30


def _cparams(sem):
    return pltpu.CompilerParams(dimension_semantics=sem, vmem_limit_bytes=VMEM_LIMIT)


def _dot(a, b):
    return jnp.dot(a, b, preferred_element_type=F32)


def _dot_nt(a, b):
    return lax.dot_general(a, b, (((1,), (1,)), ((), ())), preferred_element_type=F32)


def _dot_tn(a, b):
    return lax.dot_general(a, b, (((0,), (0,)), ((), ())), preferred_element_type=F32)


def _silu(t):
    return t * jax.nn.sigmoid(t)


def _split3(t):
    hi = t.astype(BF16)
    r1 = t - hi.astype(F32)
    mid = r1.astype(BF16)
    lo = (r1 - mid.astype(F32)).astype(BF16)
    return hi, mid, lo


def _modnorm(t, g, scale, shift):
    var = jnp.mean(t * t, axis=-1, keepdims=True)
    return t * lax.rsqrt(var + EPS) * g * (1.0 + scale) + shift


def _mod_kernel(c_ref, w_ref, b_ref, o_ref):
    a = _silu(c_ref[...])
    w = w_ref[0]
    a_hi = a.astype(BF16)
    a_lo = (a - a_hi.astype(F32)).astype(BF16)
    w_hi = w.astype(BF16)
    w_lo = (w - w_hi.astype(F32)).astype(BF16)
    o_ref[0] = _dot(a_hi, w_hi) + _dot(a_hi, w_lo) + _dot(a_lo, w_hi) + b_ref[0]


def _modulation(c, w_ada, b_ada):
    depth, d, n3 = w_ada.shape
    bsz = c.shape[0]
    rows = 16
    c_pad = jnp.zeros((rows, d), F32).at[:bsz].set(c)
    tn = 1024
    out = pl.pallas_call(
        _mod_kernel,
        grid=(depth, n3 // tn),
        in_specs=[pl.BlockSpec((rows, d), lambda l, j: (0, 0)),
                  pl.BlockSpec((1, d, tn), lambda l, j: (l, 0, j)),
                  pl.BlockSpec((1, 1, tn), lambda l, j: (l, 0, j))],
        out_specs=pl.BlockSpec((1, rows, tn), lambda l, j: (l, 0, j)),
        out_shape=jax.ShapeDtypeStruct((depth, rows, n3), F32),
        compiler_params=_cparams(("parallel", "parallel")),
        name="adaln_mod",
    )(c_pad, w_ada, b_ada.reshape(depth, 1, n3))
    return out[:, :bsz]


def _rope_kernel(pos_ref, freq_ref, cos_ref, sin_ref):
    pos = pos_ref[0].astype(F32)
    ang = freq_ref[...] * pos
    c = jnp.cos(ang)
    s = jnp.sin(ang)
    cos_ref[...] = jnp.concatenate([c, c], axis=0).T
    sin_ref[...] = jnp.concatenate([-s, s], axis=0).T


def _rope_tables(positions):
    bsz, seq = positions.shape
    half = RET_HEAD_DIM // 2
    freq = (ROPE_BASE ** (-jnp.arange(half, dtype=F32) / half)).reshape(half, 1)
    tm = 512
    nt = seq // tm
    shp = jax.ShapeDtypeStruct((bsz * seq, RET_HEAD_DIM), F32)
    return pl.pallas_call(
        _rope_kernel,
        grid=(bsz, nt),
        in_specs=[pl.BlockSpec((1, 1, tm), lambda b, i: (b, 0, i)),
                  pl.BlockSpec((half, 1), lambda b, i: (0, 0))],
        out_specs=[pl.BlockSpec((tm, RET_HEAD_DIM), lambda b, i: (b * nt + i, 0)),
                   pl.BlockSpec((tm, RET_HEAD_DIM), lambda b, i: (b * nt + i, 0))],
        out_shape=[shp, shp],
        compiler_params=_cparams(("parallel", "parallel")),
        name="rope_tables",
    )(positions.reshape(bsz, 1, seq), freq)


def _norm_kernel(x_ref, g_ref, sc_ref, sh_ref, h_ref):
    h_ref[...] = _modnorm(x_ref[...], g_ref[...], sc_ref[0], sh_ref[0]).astype(h_ref.dtype)


def _first_norm(x2, g, scale, shift, seq):
    t, d = x2.shape
    tm = min(TM_NORM, seq)
    per_b = seq // tm
    return pl.pallas_call(
        _norm_kernel,
        grid=(t // tm,),
        in_specs=[pl.BlockSpec((tm, d), lambda i: (i, 0)),
                  pl.BlockSpec((1, d), lambda i: (0, 0)),
                  pl.BlockSpec((1, 1, d), lambda i: (i // per_b, 0, 0)),
                  pl.BlockSpec((1, 1, d), lambda i: (i // per_b, 0, 0))],
        out_specs=pl.BlockSpec((tm, d), lambda i: (i, 0)),
        out_shape=jax.ShapeDtypeStruct((t, d), BF16),
        compiler_params=_cparams(("parallel",)),
        name="first_norm",
    )(x2, g.reshape(1, d), scale, shift)


def _inproj_kernel(h_ref, w_ref, ws_ref, o_ref, os_ref):
    h = h_ref[...]
    o_ref[...] = _dot(h, w_ref[...]).astype(o_ref.dtype)

    @pl.when(pl.program_id(1) == 0)
    def _():
        os_ref[...] = _dot(h, ws_ref[...])


def _in_projection(h, w_main, w_small):
    t, d = h.shape
    tm = min(TM_PROJ, t)
    return pl.pallas_call(
        _inproj_kernel,
        grid=(t // tm, N_MAIN // TN_PROJ),
        in_specs=[pl.BlockSpec((tm, d), lambda i, j: (i, 0)),
                  pl.BlockSpec((d, TN_PROJ), lambda i, j: (0, j)),
                  pl.BlockSpec((d, SMALL_W), lambda i, j: (0, 0))],
        out_specs=[pl.BlockSpec((tm, TN_PROJ), lambda i, j: (i, j)),
                   pl.BlockSpec((tm, SMALL_W), lambda i, j: (i, 0))],
        out_shape=[jax.ShapeDtypeStruct((t, N_MAIN), BF16),
                   jax.ShapeDtypeStruct((t, SMALL_W), F32)],
        compiler_params=_cparams(("parallel", "arbitrary")),
        name="in_proj",
    )(h, w_main, w_small)


def _retention_consts():
    h = np.arange(RET_HEADS, dtype=np.float64)
    log_g = np.log(1.0 - 2.0 ** (-5.0 - h))
    idx = np.arange(CHUNK, dtype=np.float64)
    diff = idx[:, None] - idx[None, :]
    intra = np.where(diff >= 0, np.exp(log_g[:, None, None] * np.maximum(diff, 0.0)), 0.0)
    dq = np.exp(log_g[:, None] * (idx + 1.0))
    dk = np.exp(log_g[:, None] * (CHUNK - 1.0 - idx))
    dq = np.broadcast_to(dq[:, :, None], (RET_HEADS, CHUNK, RET_HEAD_DIM))
    dk = np.broadcast_to(dk[:, :, None], (RET_HEADS, CHUNK, RET_HEAD_DIM))
    dchunk = tuple(float(v) for v in np.exp(log_g * CHUNK))
    return (jnp.asarray(intra, F32), jnp.asarray(dq, F32), jnp.asarray(dk, F32), dchunk)


def _ret_kernel(q_ref, k_ref, v_ref, g_ref, cos_ref, sin_ref, di_ref, dq_ref, dk_ref,
                o_ref, s_ref, *, nchunk, dchunk):
    @pl.when(pl.program_id(1) == 0)
    def _():
        s_ref[...] = jnp.zeros_like(s_ref)

    kscale = RET_HEAD_DIM ** -0.5

    def body(c, carry):
        rows = pl.ds(pl.multiple_of(c * CHUNK, CHUNK), CHUNK)
        cs = cos_ref[rows, :]
        sn = sin_ref[rows, :]
        for h in range(RET_HEADS):
            cols = slice(h * RET_HEAD_DIM, (h + 1) * RET_HEAD_DIM)
            q = q_ref[rows, cols].astype(F32)
            k = k_ref[rows, cols].astype(F32)
            v = v_ref[rows, cols]
            qr = q * cs + pltpu.roll(q, RET_HEAD_DIM // 2, 1) * sn
            kr = (k * cs + pltpu.roll(k, RET_HEAD_DIM // 2, 1) * sn) * kscale
            qb = qr.astype(BF16)
            kb = kr.astype(BF16)
            state = s_ref[h]
            s = _dot_nt(qb, kb) * di_ref[h]
            o = _dot(s.astype(BF16), v) + _dot(qb, state.astype(BF16)) * dq_ref[h]
            kd = (kr * dk_ref[h]).astype(BF16)
            s_ref[h] = state * dchunk[h] + _dot_tn(kd, v)
            o = o * lax.rsqrt(jnp.mean(o * o, axis=-1, keepdims=True) + EPS)
            o = o * _silu(g_ref[rows, cols].astype(F32))
            o_ref[rows, cols] = o.astype(o_ref.dtype)
        return carry

    lax.fori_loop(0, nchunk, body, 0, unroll=4)


def _retention(proj, cos_t, sin_t, consts, bsz, seq):
    t = proj.shape[0]
    tb = min(TB_RET, seq)
    per_b = seq // tb
    intra, dq, dk, dchunk = consts
    row = lambda b, j: b * per_b + j
    cblk = lambda idx: pl.BlockSpec((tb, RET_WIDTH), lambda b, j: (row(b, j), idx))
    const3 = pl.BlockSpec((RET_HEADS, CHUNK, RET_HEAD_DIM), lambda b, j: (0, 0, 0))
    tblk = pl.BlockSpec((tb, RET_HEAD_DIM), lambda b, j: (row(b, j), 0))
    base = COL_RET // RET_WIDTH
    return pl.pallas_call(
        functools.partial(_ret_kernel, nchunk=tb // CHUNK, dchunk=dchunk),
        grid=(bsz, per_b),
        in_specs=[cblk(base), cblk(base + 1), cblk(base + 2), cblk(base + 3),
                  tblk, tblk, const3, const3, const3],
        out_specs=pl.BlockSpec((tb, RET_WIDTH), lambda b, j: (row(b, j), 0)),
        out_shape=jax.ShapeDtypeStruct((t, RET_WIDTH), BF16),
        scratch_shapes=[pltpu.VMEM((RET_HEADS, RET_HEAD_DIM, RET_HEAD_DIM), F32)],
        compiler_params=_cparams(("parallel", "arbitrary")),
        name="retention",
    )(proj, proj, proj, proj, cos_t, sin_t, intra, dq, dk)


def _ssd_kernel(xs_ref, z_ref, b_ref, c_ref, sm_ref, cw_ref, cb_ref, bias_ref, alog_ref,
                dskip_ref, ng_ref, tril_ref, shift_ref, y_ref, f_ref,
                ubuf, st_ref, ybuf2, fcar, xcf2, xsb2, inter2, *, tb):
    j = pl.program_id(1)

    @pl.when(j == 0)
    def _():
        ubuf[0:CHUNK, :] = jnp.zeros((CHUNK, CONV_DIM), BF16)
        st_ref[...] = jnp.zeros_like(st_ref)
        fcar[...] = jnp.zeros_like(fcar)

    @pl.when(j > 0)
    def _():
        ubuf[0:CHUNK, :] = ubuf[tb:tb + CHUNK, :]

    ubuf[CHUNK:CHUNK + tb, 0:SSD_WIDTH] = xs_ref[...]
    ubuf[CHUNK:CHUNK + tb, SSD_WIDTH:SSD_WIDTH + BC_WIDTH] = b_ref[...]
    ubuf[CHUNK:CHUNK + tb, SSD_WIDTH + BC_WIDTH:CONV_DIM] = c_ref[...]

    slab = 2 * LANES

    def conv_silu(c, xcf, xsb):
        for s0 in range(0, CONV_DIM, slab):
            cols = slice(s0, s0 + slab)
            win = ubuf[c * CHUNK:(c + 2) * CHUNK, cols]
            acc = cb_ref[:, cols] + cw_ref[CONV_K - 1:CONV_K, cols] * win[CHUNK:2 * CHUNK, :].astype(F32)
            for sh in range(1, CONV_K):
                tap = CONV_K - 1 - sh
                acc = acc + cw_ref[tap:tap + 1, cols] * _dot(shift_ref[sh - 1], win)
            xc = _silu(acc)
            xcf[:, cols] = xc
            if s0 < SSD_WIDTH:
                xsb[:, cols] = xc.astype(BF16)

    lane = lax.broadcasted_iota(jnp.int32, (1, LANES), 1)
    is_f = lane < LANE_DT
    half_lane = lane < SSD_HEAD_DIM
    a_row = jnp.where((lane >= LANE_DT) & (lane < LANE_DT + SSD_HEADS), -jnp.exp(alog_ref[...]), 0.0)
    ri = lax.broadcasted_iota(jnp.int32, (CHUNK, CHUNK), 0)
    ci = lax.broadcasted_iota(jnp.int32, (CHUNK, CHUNK), 1)
    causal = ri >= ci
    tril = tril_ref[...]

    for c in range(tb // CHUNK):
        r0 = c * CHUNK
        xcf, xsb, ybuf, inter = xcf2.at[c % 2], xsb2.at[c % 2], ybuf2.at[c % 2], inter2.at[c % 2]
        conv_silu(c, xcf, xsb)

        pre = sm_ref[r0:r0 + CHUNK, :] + bias_ref[...]
        tail_term = jnp.log1p(jnp.exp(-jnp.abs(pre)))
        sp = jnp.maximum(pre, 0.0) + tail_term
        val = jnp.where(is_f, -(jnp.maximum(-pre, 0.0) + tail_term), sp * a_row)
        v_hi, v_mid, v_lo = _split3(val)
        cum = _dot(tril, v_hi) + _dot(tril, v_mid) + _dot(tril, v_lo)
        cum = cum + fcar[...]
        fcar[...] = jnp.where(is_f, cum[CHUNK - 1:CHUNK, :], 0.0)
        cum_t = cum.T
        dt_t = sp.T
        n_hi, n_mid, n_lo = _split3(jnp.where(is_f, cum * -LOG2E, 0.0))
        f_ref[r0:r0 + CHUNK, :] = (n_hi.astype(F32) + pltpu.roll(n_mid.astype(F32), FOX_HEADS, 1)
                                   + pltpu.roll(n_lo.astype(F32), 2 * FOX_HEADS, 1)).astype(BF16)
        e_col = jnp.exp(cum)

        for g in range(SSD_GROUPS):
            bcol = SSD_WIDTH + g * SSD_STATE
            bm = xcf[:, bcol:bcol + SSD_STATE]
            cm_b = xcf[:, bcol + BC_WIDTH:bcol + BC_WIDTH + SSD_STATE].astype(BF16)
            cb = _dot_nt(cm_b, bm.astype(BF16))
            bm_t = bm.T
            inter[...] = _dot(cm_b, st_ref[g].astype(BF16))
            for pr in range(SSD_HEADS_PER_GROUP // 2):
                gp = g * (SSD_HEADS_PER_GROUP // 2) + pr
                xp_b = xsb[:, gp * LANES:(gp + 1) * LANES]
                ys = []
                ups = []
                ecs = []
                els = []
                for hh in range(2):
                    ln = LANE_DT + 2 * gp + hh
                    a_c = cum[:, ln:ln + 1]
                    a_r = cum_t[ln:ln + 1, :]
                    d_r = dt_t[ln:ln + 1, :]
                    lm = jnp.where(causal, jnp.exp(a_c - a_r), 0.0)
                    m = cb * lm * d_r
                    ys.append(_dot(m.astype(BF16), xp_b))
                    last = a_r[:, CHUNK - 1:CHUNK]
                    w_r = d_r * jnp.exp(last - a_r)
                    ups.append(_dot((bm_t * w_r).astype(BF16), xp_b))
                    ecs.append(e_col[:, ln:ln + 1])
                    els.append(jnp.exp(last))
                lanes = slice(pr * LANES, (pr + 1) * LANES)
                y_pair = (jnp.where(half_lane, ys[0], ys[1])
                          + jnp.where(half_lane, ecs[0], ecs[1]) * inter[:, lanes])
                ybuf[:, gp * LANES:(gp + 1) * LANES] = y_pair
                st_ref[g, :, lanes] = (st_ref[g, :, lanes] * jnp.where(half_lane, els[0], els[1])
                                       + jnp.where(half_lane, ups[0], ups[1]))

        ss = jnp.zeros((CHUNK, 1), F32)
        for s0 in range(0, SSD_WIDTH, slab):
            cols = slice(s0, s0 + slab)
            gated = ((ybuf[:, cols] + dskip_ref[:, cols] * xcf[:, cols])
                     * _silu(z_ref[r0:r0 + CHUNK, cols].astype(F32)))
            ybuf[:, cols] = gated
            ss = ss + jnp.sum(gated * gated, axis=-1, keepdims=True)
        inv = lax.rsqrt(ss * (1.0 / SSD_WIDTH) + EPS)
        for s0 in range(0, SSD_WIDTH, slab):
            cols = slice(s0, s0 + slab)
            y_ref[r0:r0 + CHUNK, cols] = (ybuf[:, cols] * inv * ng_ref[:, cols]).astype(y_ref.dtype)


def _shift_matrices():
    m = np.zeros((CONV_K - 1, CHUNK, 2 * CHUNK), np.float32)
    t = np.arange(CHUNK)
    for sh in range(1, CONV_K):
        m[sh - 1, t, CHUNK + t - sh] = 1.0
    return jnp.asarray(m, BF16)


def _ssd(proj, small, conv_w, conv_b, bias_row, alog_row, dskip_row, norm_g, tril, shifts, bsz, seq):
    t = proj.shape[0]
    tb = min(TB_SSD, seq)
    per_b = seq // tb
    row = lambda b, j: b * per_b + j
    full = lambda shape: pl.BlockSpec(shape, lambda b, j: (0,) * len(shape))
    return pl.pallas_call(
        functools.partial(_ssd_kernel, tb=tb),
        grid=(bsz, per_b),
        in_specs=[pl.BlockSpec((tb, SSD_WIDTH), lambda b, j: (row(b, j), COL_XS // SSD_WIDTH)),
                  pl.BlockSpec((tb, SSD_WIDTH), lambda b, j: (row(b, j), COL_Z // SSD_WIDTH)),
                  pl.BlockSpec((tb, BC_WIDTH), lambda b, j: (row(b, j), COL_B // BC_WIDTH)),
                  pl.BlockSpec((tb, BC_WIDTH), lambda b, j: (row(b, j), COL_C // BC_WIDTH)),
                  pl.BlockSpec((tb, SMALL_W), lambda b, j: (row(b, j), 0)),
                  full((CONV_K, CONV_DIM)), full((1, CONV_DIM)), full((1, SMALL_W)),
                  full((1, SMALL_W)), full((1, SSD_WIDTH)), full((1, SSD_WIDTH)),
                  full((CHUNK, CHUNK)), full((CONV_K - 1, CHUNK, 2 * CHUNK))],
        out_specs=[pl.BlockSpec((tb, SSD_WIDTH), lambda b, j: (row(b, j), 0)),
                   pl.BlockSpec((tb, SMALL_W), lambda b, j: (row(b, j), 0))],
        out_shape=[jax.ShapeDtypeStruct((t, SSD_WIDTH), BF16),
                   jax.ShapeDtypeStruct((t, SMALL_W), BF16)],
        scratch_shapes=[pltpu.VMEM((CHUNK + tb, CONV_DIM), BF16),
                        pltpu.VMEM((SSD_GROUPS, SSD_STATE, SSD_HEADS_PER_GROUP * SSD_HEAD_DIM), F32),
                        pltpu.VMEM((2, CHUNK, SSD_WIDTH), F32),
                        pltpu.VMEM((1, SMALL_W), F32),
                        pltpu.VMEM((2, CHUNK, CONV_DIM), F32),
                        pltpu.VMEM((2, CHUNK, SSD_WIDTH), BF16),
                        pltpu.VMEM((2, CHUNK, SSD_HEADS_PER_GROUP * SSD_HEAD_DIM), F32)],
        compiler_params=_cparams(("parallel", "arbitrary")),
        name="ssd",
    )(proj, proj, proj, proj, small, conv_w, conv_b, bias_row, alog_row, dskip_row, norm_g, tril,
      shifts)


def _fox_bounds_kernel(q_ref, k_ref, nf_ref, nft_ref, o_ref, *, tq, nq):
    pair = pl.program_id(1)
    rows = 16
    hrow = lax.broadcasted_iota(jnp.int32, (rows, LANES), 0)
    hlane = lax.broadcasted_iota(jnp.int32, (rows, LANES), 1)
    selector = jnp.where((hrow < 2) & ((hlane >= FOX_HEAD_DIM) == (hrow == 1)), 1.0, 0.0).astype(BF16)
    q = q_ref[...]
    k = k_ref[...]
    qn2 = _dot_nt(selector, q * q)
    kn2 = jnp.max(_dot_nt(selector, k * k), axis=1, keepdims=True)
    diag = _dot_nt(selector, q * k)
    qmax2 = jnp.zeros((rows, LANES), F32)
    dmin = jnp.zeros((rows, LANES), F32)
    for t in range(nq):
        tile = slice(t * tq, (t + 1) * tq)
        qmax2 = jnp.where(hlane == t, jnp.max(qn2[:, tile], axis=1, keepdims=True), qmax2)
        dmin = jnp.where(hlane == t, jnp.min(diag[:, tile], axis=1, keepdims=True), dmin)
    qk = jnp.sqrt(qmax2 * kn2) * NORM_SLACK

    nft = nft_ref[0]
    nft_prev = pltpu.roll(nft, 1, 1)
    nf = nf_ref[0]
    sub8 = lax.broadcasted_iota(jnp.int32, (FOX_HEADS, LANES), 0)
    lane8 = lax.broadcasted_iota(jnp.int32, (nq, FOX_HEADS), 1)
    kb_idx = lax.broadcasted_iota(jnp.int32, (nq, LANES), 0)
    t_idx = lax.broadcasted_iota(jnp.int32, (nq, LANES), 1)
    first = None
    for h in range(2):
        hd = 2 * pair + h
        prev_row = jnp.sum(jnp.where(sub8 == hd, nft_prev, 0.0), axis=0, keepdims=True)
        nf_col = jnp.sum(jnp.where(lane8 == hd, nf, 0.0), axis=1, keepdims=True)
        thresh = prev_row + dmin[h:h + 1, :] - NORM_SLACK * qk[h:h + 1, :] - PRUNE_MARGIN
        dead = (nf_col <= thresh) & (kb_idx < t_idx)
        count = jnp.sum(jnp.where(dead, 1.0, 0.0), axis=0, keepdims=True)
        first = count if first is None else jnp.minimum(first, count)
    o_ref[0, 0] = jnp.broadcast_to(first, (FOX_HEADS, LANES)).astype(jnp.int32)


def _fox_bounds(proj, f_aug, bsz, seq, tq):
    nq = seq // tq
    base = COL_FOX // LANES
    nblk = FOX_WIDTH // LANES
    ends = f_aug.reshape(bsz, nq, tq, SMALL_W)[:, :, tq - 1, 0:3 * FOX_HEADS].astype(F32)
    nf = ends.reshape(bsz, nq, 3, FOX_HEADS).sum(axis=2)
    nft = jnp.full((bsz, FOX_HEADS, LANES), jnp.inf, F32).at[:, :, 0:nq].set(nf.transpose(0, 2, 1))
    out = pl.pallas_call(
        functools.partial(_fox_bounds_kernel, tq=tq, nq=nq),
        grid=(bsz, FOX_PAIRS),
        in_specs=[pl.BlockSpec((seq, LANES), lambda b, p: (b, base + p)),
                  pl.BlockSpec((seq, LANES), lambda b, p: (b, base + nblk + p)),
                  pl.BlockSpec((1, nq, FOX_HEADS), lambda b, p: (b, 0, 0)),
                  pl.BlockSpec((1, FOX_HEADS, LANES), lambda b, p: (b, 0, 0))],
        out_specs=pl.BlockSpec((1, 1, FOX_HEADS, LANES), lambda b, p: (b, p, 0, 0)),
        out_shape=jax.ShapeDtypeStruct((bsz, FOX_PAIRS, FOX_HEADS, LANES), jnp.int32),
        compiler_params=_cparams(("parallel", "parallel")),
        name="fox_bounds",
    )(proj, proj, nf, nft)
    return out[:, :, 0, 0:nq].reshape(-1)


def _fox_kernel(start_ref, q_ref, k_ref, v_ref, g_ref, f_ref, o_ref, kaug, vaug, m_ref, acc_ref,
                *, tq, nq):
    bat = pl.program_id(0)
    pair = pl.program_id(1)
    i = pl.program_id(2)
    lane = lax.broadcasted_iota(jnp.int32, (1, LANES), 1)
    head0 = lane < FOX_HEAD_DIM

    @pl.when(i == 0)
    def _():
        kaug[:, 0:LANES] = k_ref[...]
        kaug[:, LANES:2 * LANES] = f_ref[...]
        vaug[:, 0:LANES] = v_ref[...]
        vaug[:, LANES:2 * LANES] = jnp.ones((vaug.shape[0], LANES), BF16)

    q = q_ref[...]
    zero = jnp.zeros_like(q)
    qh = []
    for h in range(2):
        hd = 2 * pair + h
        sel = (lane == hd) | (lane == hd + FOX_HEADS) | (lane == hd + 2 * FOX_HEADS)
        ones = jnp.broadcast_to(jnp.where(sel, 1.0, 0.0).astype(BF16), (tq, LANES))
        qm = jnp.where(head0, q, zero) if h == 0 else jnp.where(head0, zero, q)
        qh.append(jnp.concatenate([qm, ones], axis=1))
    start = start_ref[(bat * FOX_PAIRS + pair) * nq + i]

    m_ref[...] = jnp.full(m_ref.shape, NEG_BIG, F32)
    acc_ref[...] = jnp.zeros_like(acc_ref)

    def block(kb, masked):
        cols = pl.ds(pl.multiple_of(kb * tq, tq), tq)
        k = kaug[cols, :]
        v = vaug[cols, :]
        if masked:
            ri = lax.broadcasted_iota(jnp.int32, (tq, tq), 0)
            ci = lax.broadcasted_iota(jnp.int32, (tq, tq), 1)
            keep = ri >= ci
        for h in range(2):
            s = _dot_nt(qh[h], k)
            if masked:
                s = jnp.where(keep, s, NEG_BIG)
            m_old = m_ref[h]
            m_new = jnp.maximum(m_old, jnp.max(s, axis=-1, keepdims=True))
            alpha = jnp.exp2(m_old - m_new)
            p = jnp.exp2(s - jnp.concatenate([m_new] * (tq // LANES), axis=1))
            acc_ref[h] = (jnp.concatenate([alpha, alpha], axis=1) * acc_ref[h]
                          + _dot(p.astype(BF16), v))
            m_ref[h] = m_new

    def body(kq, carry):
        for u in range(KV_UNROLL):
            block(start + KV_UNROLL * kq + u, False)
        return carry

    full = (i - start) // KV_UNROLL
    lax.fori_loop(0, full, body, 0)
    tail0 = start + full * KV_UNROLL
    for extra in range(KV_UNROLL):
        @pl.when(i - tail0 == extra)
        def _(extra=extra):
            for u in range(extra):
                block(tail0 + u, False)
            block(i, True)

    o = jnp.where(head0, acc_ref[0, :, 0:LANES] / acc_ref[0, :, LANES:2 * LANES],
                  acc_ref[1, :, 0:LANES] / acc_ref[1, :, LANES:2 * LANES])
    o = o * _silu(g_ref[...].astype(F32))
    o_ref[...] = o.astype(o_ref.dtype)


def _fox(proj, f_aug, bsz, seq):
    t = proj.shape[0]
    tq = min(T_ATT, seq)
    nq = seq // tq
    base = COL_FOX // LANES
    nblk = FOX_WIDTH // LANES
    starts = _fox_bounds(proj, f_aug, bsz, seq, tq)
    return pl.pallas_call(
        functools.partial(_fox_kernel, tq=tq, nq=nq),
        grid=(bsz, FOX_PAIRS, nq),
        in_specs=[pl.BlockSpec(memory_space=pltpu.SMEM),
                  pl.BlockSpec((tq, LANES), lambda b, p, i: (b * nq + i, base + p)),
                  pl.BlockSpec((seq, LANES), lambda b, p, i: (b, base + nblk + p)),
                  pl.BlockSpec((seq, LANES), lambda b, p, i: (b, base + 2 * nblk + p)),
                  pl.BlockSpec((tq, LANES), lambda b, p, i: (b * nq + i, base + 3 * nblk + p)),
                  pl.BlockSpec((seq, SMALL_W), lambda b, p, i: (b, 0))],
        out_specs=pl.BlockSpec((tq, LANES), lambda b, p, i: (b * nq + i, p)),
        out_shape=jax.ShapeDtypeStruct((t, FOX_WIDTH), BF16),
        scratch_shapes=[pltpu.VMEM((seq, 2 * LANES), BF16),
                        pltpu.VMEM((seq, 2 * LANES), BF16),
                        pltpu.VMEM((2, tq, LANES), F32),
                        pltpu.VMEM((2, tq, 2 * LANES), F32)],
        compiler_params=_cparams(("parallel", "parallel", "arbitrary")),
        name="fox_attention",
    )(starts, proj, proj, proj, proj, f_aug)


def _out_kernel(ret_ref, ssd_ref, fox_ref, w_ref, x_ref, gate_ref, g_ref, sc_ref, sh_ref,
                *out_refs, emit_x):
    acc = _dot(ret_ref[...], w_ref[0:RET_WIDTH, :])
    acc = acc + _dot(ssd_ref[...], w_ref[RET_WIDTH:RET_WIDTH + SSD_WIDTH, :])
    acc = acc + _dot(fox_ref[...], w_ref[RET_WIDTH + SSD_WIDTH:MIX_WIDTH, :])
    xn = x_ref[...] + gate_ref[0] * acc
    if emit_x:
        out_refs[0][...] = xn
    y_ref = out_refs[-1]
    y_ref[...] = _modnorm(xn, g_ref[...], sc_ref[0], sh_ref[0]).astype(y_ref.dtype)


def _out_projection(ret, ssd, fox, w_out, x2, gate, g, scale, shift, seq, y_dtype, emit_x):
    t, d = x2.shape
    tm = min(TM_OUT, seq)
    per_b = seq // tm
    rowblk = lambda w: pl.BlockSpec((tm, w), lambda i: (i, 0))
    perb = pl.BlockSpec((1, 1, d), lambda i: (i // per_b, 0, 0))
    out_specs = [rowblk(d)]
    out_shape = [jax.ShapeDtypeStruct((t, d), y_dtype)]
    if emit_x:
        out_specs = [rowblk(d)] + out_specs
        out_shape = [jax.ShapeDtypeStruct((t, d), F32)] + out_shape
    return pl.pallas_call(
        functools.partial(_out_kernel, emit_x=emit_x),
        grid=(t // tm,),
        in_specs=[rowblk(RET_WIDTH), rowblk(SSD_WIDTH), rowblk(FOX_WIDTH),
                  pl.BlockSpec((MIX_WIDTH, d), lambda i: (0, 0)),
                  rowblk(d), perb, pl.BlockSpec((1, d), lambda i: (0, 0)), perb, perb],
        out_specs=out_specs,
        out_shape=out_shape,
        compiler_params=_cparams(("parallel",)),
        name="out_proj",
    )(ret, ssd, fox, w_out, x2, gate, g.reshape(1, d), scale, shift)


def _permute_in_weights(w_in):
    o_xbc = 4 * RET_WIDTH
    o_dt = o_xbc + CONV_DIM
    o_z = o_dt + SSD_HEADS
    o_fox = o_z + SSD_WIDTH
    o_f = o_fox + 4 * FOX_WIDTH
    col_scale = np.ones((w_in.shape[-1],), np.float32)
    col_scale[o_fox:o_fox + FOX_WIDTH] = FOX_Q_SCALE
    wb = (w_in * col_scale).astype(BF16)
    main = jnp.concatenate([
        wb[..., 0:o_xbc + SSD_WIDTH],
        wb[..., o_z:o_z + SSD_WIDTH],
        wb[..., o_xbc + SSD_WIDTH:o_dt],
        wb[..., o_fox:o_f],
    ], axis=-1)
    pad = jnp.zeros(wb.shape[:-1] + (SMALL_W - FOX_HEADS - SSD_HEADS,), BF16)
    small = jnp.concatenate([wb[..., o_f:o_f + FOX_HEADS], wb[..., o_dt:o_z], pad], axis=-1)
    return main, small


def _lane_row(depth, pieces):
    row = jnp.zeros((depth, 1, SMALL_W), F32)
    for start, arr in pieces:
        row = row.at[:, 0, start:start + arr.shape[-1]].set(arr.astype(F32))
    return row


def kernel(x, c, positions, norm_g, w_ada, b_ada, w_in, conv_w, conv_b, dt_bias, a_log, d_skip,
           ssd_norm_g, b_forget, w_out, final_g):
    bsz, seq, d = x.shape
    depth = w_in.shape[0]
    t = bsz * seq

    mod = _modulation(c, w_ada, b_ada)
    shift = mod[:, :, 0:d].reshape(depth, bsz, 1, d)
    scale = mod[:, :, d:2 * d].reshape(depth, bsz, 1, d)
    gate = mod[:, :, 2 * d:3 * d].reshape(depth, bsz, 1, d)

    cos_t, sin_t = _rope_tables(positions)
    w_main, w_small = _permute_in_weights(w_in)
    w_out_b = w_out.astype(BF16)
    ret_consts = _retention_consts()
    tril = jnp.asarray(np.tril(np.ones((CHUNK, CHUNK), np.float32)), BF16)
    shifts = _shift_matrices()
    bias_rows = _lane_row(depth, [(LANE_F, b_forget), (LANE_DT, dt_bias)])
    alog_rows = _lane_row(depth, [(LANE_DT, a_log)])
    dskip_rows = jnp.repeat(d_skip, SSD_HEAD_DIM, axis=-1).reshape(depth, 1, SSD_WIDTH)
    zeros_bd = jnp.zeros((bsz, 1, d), F32)

    x2 = x.reshape(t, d)
    h = _first_norm(x2, norm_g[0], scale[0], shift[0], seq)
    out = None
    for l in range(depth):
        proj, small = _in_projection(h, w_main[l], w_small[l])
        ret = _retention(proj, cos_t, sin_t, ret_consts, bsz, seq)
        ssd, f_rows = _ssd(proj, small, conv_w[l], conv_b[l].reshape(1, CONV_DIM), bias_rows[l],
                           alog_rows[l], dskip_rows[l], ssd_norm_g[l].reshape(1, SSD_WIDTH), tril,
                           shifts, bsz, seq)
        fox = _fox(proj, f_rows, bsz, seq)
        if l + 1 < depth:
            x2, h = _out_projection(ret, ssd, fox, w_out_b[l], x2, gate[l], norm_g[l + 1],
                                    scale[l + 1], shift[l + 1], seq, BF16, True)
        else:
            (out,) = _out_projection(ret, ssd, fox, w_out_b[l], x2, gate[l], final_g,
                                     zeros_bd, zeros_bd, seq, F32, False)
    return out.reshape(bsz, seq, d)
```

```python
import functools
import math

import numpy as np
import jax
import jax.numpy as jnp
from jax import lax
from jax.experimental import pallas as pl
from jax.experimental.pallas import tpu as pltpu

F32 = jnp.float32
BF16 = jnp.bfloat16

RET_HEADS = 4
RET_HEAD_DIM = 128
RET_WIDTH = RET_HEADS * RET_HEAD_DIM
SSD_HEADS = 16
SSD_HEAD_DIM = 64
SSD_WIDTH = SSD_HEADS * SSD_HEAD_DIM
SSD_GROUPS = 2
SSD_STATE = 128
SSD_HEADS_PER_GROUP = SSD_HEADS // SSD_GROUPS
CONV_K = 4
BC_WIDTH = SSD_GROUPS * SSD_STATE
CONV_DIM = SSD_WIDTH + 2 * BC_WIDTH
FOX_HEADS = 8
FOX_HEAD_DIM = 64
FOX_WIDTH = FOX_HEADS * FOX_HEAD_DIM
FOX_PAIRS = FOX_HEADS // 2
MIX_WIDTH = RET_WIDTH + SSD_WIDTH + FOX_WIDTH
CHUNK = 128
ROPE_BASE = 10000.0
EPS = 1e-6
LOG2E = math.log2(math.e)
FOX_Q_SCALE = FOX_HEAD_DIM ** -0.5 * LOG2E

COL_RET = 0
COL_XS = 4 * RET_WIDTH
COL_Z = COL_XS + SSD_WIDTH
COL_B = COL_Z + SSD_WIDTH
COL_C = COL_B + BC_WIDTH
COL_FOX = COL_C + BC_WIDTH
N_MAIN = COL_FOX + 4 * FOX_WIDTH
SMALL_W = 128
LANE_F = 0
LANE_DT = FOX_HEADS

LANES = 128
TM_PROJ = 2048
TN_PROJ = 1664
TM_OUT = 512
TM_NORM = 1024
TB_RET = 1024
TB_SSD = 1024
T_ATT = 512
KV_UNROLL = 4
PRUNE_MARGIN = 152.0
NORM_SLACK = 1.02
VMEM_LIMIT = 48 * 1024 * 1024
NEG_BIG = -1e30


def _cparams(sem):
    return pltpu.CompilerParams(dimension_semantics=sem, vmem_limit_bytes=VMEM_LIMIT)


def _dot(a, b):
    return jnp.dot(a, b, preferred_element_type=F32)


def _dot_nt(a, b):
    return lax.dot_general(a, b, (((1,), (1,)), ((), ())), preferred_element_type=F32)


def _dot_tn(a, b):
    return lax.dot_general(a, b, (((0,), (0,)), ((), ())), preferred_element_type=F32)


def _silu(t):
    return t * jax.nn.sigmoid(t)


def _split3(t):
    hi = t.astype(BF16)
    r1 = t - hi.astype(F32)
    mid = r1.astype(BF16)
    lo = (r1 - mid.astype(F32)).astype(BF16)
    return hi, mid, lo


def _modnorm(t, g, scale, shift):
    var = jnp.mean(t * t, axis=-1, keepdims=True)
    return t * lax.rsqrt(var + EPS) * g * (1.0 + scale) + shift


def _mod_kernel(c_ref, w_ref, b_ref, o_ref):
    a = _silu(c_ref[...])
    w = w_ref[0]
    a_hi = a.astype(BF16)
    a_lo = (a - a_hi.astype(F32)).astype(BF16)
    w_hi = w.astype(BF16)
    w_lo = (w - w_hi.astype(F32)).astype(BF16)
    o_ref[0] = _dot(a_hi, w_hi) + _dot(a_hi, w_lo) + _dot(a_lo, w_hi) + b_ref[0]


def _modulation(c, w_ada, b_ada):
    depth, d, n3 = w_ada.shape
    bsz = c.shape[0]
    rows = 16
    c_pad = jnp.zeros((rows, d), F32).at[:bsz].set(c)
    tn = 1024
    out = pl.pallas_call(
        _mod_kernel,
        grid=(depth, n3 // tn),
        in_specs=[pl.BlockSpec((rows, d), lambda l, j: (0, 0)),
                  pl.BlockSpec((1, d, tn), lambda l, j: (l, 0, j)),
                  pl.BlockSpec((1, 1, tn), lambda l, j: (l, 0, j))],
        out_specs=pl.BlockSpec((1, rows, tn), lambda l, j: (l, 0, j)),
        out_shape=jax.ShapeDtypeStruct((depth, rows, n3), F32),
        compiler_params=_cparams(("parallel", "parallel")),
        name="adaln_mod",
    )(c_pad, w_ada, b_ada.reshape(depth, 1, n3))
    return out[:, :bsz]


def _rope_kernel(pos_ref, freq_ref, cos_ref, sin_ref):
    pos = pos_ref[0].astype(F32)
    ang = freq_ref[...] * pos
    c = jnp.cos(ang)
    s = jnp.sin(ang)
    cos_ref[...] = jnp.concatenate([c, c], axis=0).T
    sin_ref[...] = jnp.concatenate([-s, s], axis=0).T


def _rope_tables(positions):
    bsz, seq = positions.shape
    half = RET_HEAD_DIM // 2
    freq = (ROPE_BASE ** (-jnp.arange(half, dtype=F32) / half)).reshape(half, 1)
    tm = 512
    nt = seq // tm
    shp = jax.ShapeDtypeStruct((bsz * seq, RET_HEAD_DIM), F32)
    return pl.pallas_call(
        _rope_kernel,
        grid=(bsz, nt),
        in_specs=[pl.BlockSpec((1, 1, tm), lambda b, i: (b, 0, i)),
                  pl.BlockSpec((half, 1), lambda b, i: (0, 0))],
        out_specs=[pl.BlockSpec((tm, RET_HEAD_DIM), lambda b, i: (b * nt + i, 0)),
                   pl.BlockSpec((tm, RET_HEAD_DIM), lambda b, i: (b * nt + i, 0))],
        out_shape=[shp, shp],
        compiler_params=_cparams(("parallel", "parallel")),
        name="rope_tables",
    )(positions.reshape(bsz, 1, seq), freq)


def _norm_kernel(x_ref, g_ref, sc_ref, sh_ref, h_ref):
    h_ref[...] = _modnorm(x_ref[...], g_ref[...], sc_ref[0], sh_ref[0]).astype(h_ref.dtype)


def _first_norm(x2, g, scale, shift, seq):
    t, d = x2.shape
    tm = min(TM_NORM, seq)
    per_b = seq // tm
    return pl.pallas_call(
        _norm_kernel,
        grid=(t // tm,),
        in_specs=[pl.BlockSpec((tm, d), lambda i: (i, 0)),
                  pl.BlockSpec((1, d), lambda i: (0, 0)),
                  pl.BlockSpec((1, 1, d), lambda i: (i // per_b, 0, 0)),
                  pl.BlockSpec((1, 1, d), lambda i: (i // per_b, 0, 0))],
        out_specs=pl.BlockSpec((tm, d), lambda i: (i, 0)),
        out_shape=jax.ShapeDtypeStruct((t, d), BF16),
        compiler_params=_cparams(("parallel",)),
        name="first_norm",
    )(x2, g.reshape(1, d), scale, shift)


def _inproj_kernel(h_ref, w_ref, ws_ref, o_ref, os_ref):
    h = h_ref[...]
    o_ref[...] = _dot(h, w_ref[...]).astype(o_ref.dtype)

    @pl.when(pl.program_id(1) == 0)
    def _():
        os_ref[...] = _dot(h, ws_ref[...])


def _in_projection(h, w_main, w_small, layer):
    t, d = h.shape
    tm = min(TM_PROJ, t)
    return pl.pallas_call(
        _inproj_kernel,
        grid=(t // tm, N_MAIN // TN_PROJ),
        in_specs=[pl.BlockSpec((tm, d), lambda i, j: (i, 0)),
                  pl.BlockSpec((None, d, TN_PROJ), lambda i, j: (layer, 0, j)),
                  pl.BlockSpec((None, d, SMALL_W), lambda i, j: (layer, 0, 0))],
        out_specs=[pl.BlockSpec((tm, TN_PROJ), lambda i, j: (i, j)),
                   pl.BlockSpec((tm, SMALL_W), lambda i, j: (i, 0))],
        out_shape=[jax.ShapeDtypeStruct((t, N_MAIN), BF16),
                   jax.ShapeDtypeStruct((t, SMALL_W), F32)],
        compiler_params=_cparams(("parallel", "arbitrary")),
        name="in_proj",
    )(h, w_main, w_small)


def _retention_consts():
    h = np.arange(RET_HEADS, dtype=np.float64)
    log_g = np.log(1.0 - 2.0 ** (-5.0 - h))
    idx = np.arange(CHUNK, dtype=np.float64)
    diff = idx[:, None] - idx[None, :]
    intra = np.where(diff >= 0, np.exp(log_g[:, None, None] * np.maximum(diff, 0.0)), 0.0)
    dq = np.exp(log_g[:, None] * (idx + 1.0))
    dk = np.exp(log_g[:, None] * (CHUNK - 1.0 - idx))
    dq = np.broadcast_to(dq[:, :, None], (RET_HEADS, CHUNK, RET_HEAD_DIM))
    dk = np.broadcast_to(dk[:, :, None], (RET_HEADS, CHUNK, RET_HEAD_DIM))
    dchunk = tuple(float(v) for v in np.exp(log_g * CHUNK))
    return (jnp.asarray(intra, F32), jnp.asarray(dq, F32), jnp.asarray(dk, F32), dchunk)


def _ret_kernel(q_ref, k_ref, v_ref, g_ref, cos_ref, sin_ref, di_ref, dq_ref, dk_ref,
                o_ref, s_ref, *, nchunk, dchunk):
    @pl.when(pl.program_id(1) == 0)
    def _():
        s_ref[...] = jnp.zeros_like(s_ref)

    kscale = RET_HEAD_DIM ** -0.5

    def body(c, carry):
        rows = pl.ds(pl.multiple_of(c * CHUNK, CHUNK), CHUNK)
        cs = cos_ref[rows, :]
        sn = sin_ref[rows, :]
        for h in range(RET_HEADS):
            cols = slice(h * RET_HEAD_DIM, (h + 1) * RET_HEAD_DIM)
            q = q_ref[rows, cols].astype(F32)
            k = k_ref[rows, cols].astype(F32)
            v = v_ref[rows, cols]
            qr = q * cs + pltpu.roll(q, RET_HEAD_DIM // 2, 1) * sn
            kr = (k * cs + pltpu.roll(k, RET_HEAD_DIM // 2, 1) * sn) * kscale
            qb = qr.astype(BF16)
            kb = kr.astype(BF16)
            state = s_ref[h]
            s = _dot_nt(qb, kb) * di_ref[h]
            o = _dot(s.astype(BF16), v) + _dot(qb, state.astype(BF16)) * dq_ref[h]
            kd = (kr * dk_ref[h]).astype(BF16)
            s_ref[h] = state * dchunk[h] + _dot_tn(kd, v)
            o = o * lax.rsqrt(jnp.mean(o * o, axis=-1, keepdims=True) + EPS)
            o = o * _silu(g_ref[rows, cols].astype(F32))
            o_ref[rows, cols] = o.astype(o_ref.dtype)
        return carry

    lax.fori_loop(0, nchunk, body, 0, unroll=4)


def _retention(proj, cos_t, sin_t, consts, bsz, seq):
    t = proj.shape[0]
    tb = min(TB_RET, seq)
    per_b = seq // tb
    intra, dq, dk, dchunk = consts
    row = lambda b, j: b * per_b + j
    cblk = lambda idx: pl.BlockSpec((tb, RET_WIDTH), lambda b, j: (row(b, j), idx))
    const3 = pl.BlockSpec((RET_HEADS, CHUNK, RET_HEAD_DIM), lambda b, j: (0, 0, 0))
    tblk = pl.BlockSpec((tb, RET_HEAD_DIM), lambda b, j: (row(b, j), 0))
    base = COL_RET // RET_WIDTH
    return pl.pallas_call(
        functools.partial(_ret_kernel, nchunk=tb // CHUNK, dchunk=dchunk),
        grid=(bsz, per_b),
        in_specs=[cblk(base), cblk(base + 1), cblk(base + 2), cblk(base + 3),
                  tblk, tblk, const3, const3, const3],
        out_specs=pl.BlockSpec((tb, RET_WIDTH), lambda b, j: (row(b, j), 0)),
        out_shape=jax.ShapeDtypeStruct((t, RET_WIDTH), BF16),
        scratch_shapes=[pltpu.VMEM((RET_HEADS, RET_HEAD_DIM, RET_HEAD_DIM), F32)],
        compiler_params=_cparams(("parallel", "arbitrary")),
        name="retention",
    )(proj, proj, proj, proj, cos_t, sin_t, intra, dq, dk)


def _ssd_kernel(xs_ref, z_ref, b_ref, c_ref, sm_ref, cw_ref, cb_ref, bias_ref, alog_ref,
                dskip_ref, ng_ref, tril_ref, shift_ref, y_ref, f_ref,
                ubuf, st_ref, ybuf2, fcar, xcf2, xsb2, inter2, *, tb):
    j = pl.program_id(1)

    @pl.when(j == 0)
    def _():
        ubuf[0:CHUNK, :] = jnp.zeros((CHUNK, CONV_DIM), BF16)
        st_ref[...] = jnp.zeros_like(st_ref)
        fcar[...] = jnp.zeros_like(fcar)

    @pl.when(j > 0)
    def _():
        ubuf[0:CHUNK, :] = ubuf[tb:tb + CHUNK, :]

    ubuf[CHUNK:CHUNK + tb, 0:SSD_WIDTH] = xs_ref[...]
    ubuf[CHUNK:CHUNK + tb, SSD_WIDTH:SSD_WIDTH + BC_WIDTH] = b_ref[...]
    ubuf[CHUNK:CHUNK + tb, SSD_WIDTH + BC_WIDTH:CONV_DIM] = c_ref[...]

    slab = 2 * LANES

    def conv_silu(c, xcf, xsb):
        for s0 in range(0, CONV_DIM, slab):
            cols = slice(s0, s0 + slab)
            win = ubuf[c * CHUNK:(c + 2) * CHUNK, cols]
            acc = cb_ref[:, cols] + cw_ref[CONV_K - 1:CONV_K, cols] * win[CHUNK:2 * CHUNK, :].astype(F32)
            for sh in range(1, CONV_K):
                tap = CONV_K - 1 - sh
                acc = acc + cw_ref[tap:tap + 1, cols] * _dot(shift_ref[sh - 1], win)
            xc = _silu(acc)
            xcf[:, cols] = xc
            if s0 < SSD_WIDTH:
                xsb[:, cols] = xc.astype(BF16)

    lane = lax.broadcasted_iota(jnp.int32, (1, LANES), 1)
    is_f = lane < LANE_DT
    half_lane = lane < SSD_HEAD_DIM
    a_row = jnp.where((lane >= LANE_DT) & (lane < LANE_DT + SSD_HEADS), -jnp.exp(alog_ref[...]), 0.0)
    ri = lax.broadcasted_iota(jnp.int32, (CHUNK, CHUNK), 0)
    ci = lax.broadcasted_iota(jnp.int32, (CHUNK, CHUNK), 1)
    causal = ri >= ci
    tril = tril_ref[...]

    for c in range(tb // CHUNK):
        r0 = c * CHUNK
        xcf, xsb, ybuf, inter = xcf2.at[c % 2], xsb2.at[c % 2], ybuf2.at[c % 2], inter2.at[c % 2]
        conv_silu(c, xcf, xsb)

        pre = sm_ref[r0:r0 + CHUNK, :] + bias_ref[...]
        tail_term = jnp.log1p(jnp.exp(-jnp.abs(pre)))
        sp = jnp.maximum(pre, 0.0) + tail_term
        val = jnp.where(is_f, -(jnp.maximum(-pre, 0.0) + tail_term), sp * a_row)
        v_hi, v_mid, v_lo = _split3(val)
        cum = _dot(tril, v_hi) + _dot(tril, v_mid) + _dot(tril, v_lo)
        cum = cum + fcar[...]
        fcar[...] = jnp.where(is_f, cum[CHUNK - 1:CHUNK, :], 0.0)
        cum_t = cum.T
        dt_t = sp.T
        n_hi, n_mid, n_lo = _split3(jnp.where(is_f, cum * -LOG2E, 0.0))
        f_ref[r0:r0 + CHUNK, :] = (n_hi.astype(F32) + pltpu.roll(n_mid.astype(F32), FOX_HEADS, 1)
                                   + pltpu.roll(n_lo.astype(F32), 2 * FOX_HEADS, 1)).astype(BF16)
        e_col = jnp.exp(cum)

        for g in range(SSD_GROUPS):
            bcol = SSD_WIDTH + g * SSD_STATE
            bm = xcf[:, bcol:bcol + SSD_STATE]
            cm_b = xcf[:, bcol + BC_WIDTH:bcol + BC_WIDTH + SSD_STATE].astype(BF16)
            cb = _dot_nt(cm_b, bm.astype(BF16))
            bm_t = bm.T
            inter[...] = _dot(cm_b, st_ref[g].astype(BF16))
            for pr in range(SSD_HEADS_PER_GROUP // 2):
                gp = g * (SSD_HEADS_PER_GROUP // 2) + pr
                xp_b = xsb[:, gp * LANES:(gp + 1) * LANES]
                ys = []
                ups = []
                ecs = []
                els = []
                for hh in range(2):
                    ln = LANE_DT + 2 * gp + hh
                    a_c = cum[:, ln:ln + 1]
                    a_r = cum_t[ln:ln + 1, :]
                    d_r = dt_t[ln:ln + 1, :]
                    lm = jnp.where(causal, jnp.exp(a_c - a_r), 0.0)
                    m = cb * lm * d_r
                    ys.append(_dot(m.astype(BF16), xp_b))
                    last = a_r[:, CHUNK - 1:CHUNK]
                    w_r = d_r * jnp.exp(last - a_r)
                    ups.append(_dot((bm_t * w_r).astype(BF16), xp_b))
                    ecs.append(e_col[:, ln:ln + 1])
                    els.append(jnp.exp(last))
                lanes = slice(pr * LANES, (pr + 1) * LANES)
                y_pair = (jnp.where(half_lane, ys[0], ys[1])
                          + jnp.where(half_lane, ecs[0], ecs[1]) * inter[:, lanes])
                ybuf[:, gp * LANES:(gp + 1) * LANES] = y_pair
                st_ref[g, :, lanes] = (st_ref[g, :, lanes] * jnp.where(half_lane, els[0], els[1])
                                       + jnp.where(half_lane, ups[0], ups[1]))

        ss = jnp.zeros((CHUNK, 1), F32)
        for s0 in range(0, SSD_WIDTH, slab):
            cols = slice(s0, s0 + slab)
            gated = ((ybuf[:, cols] + dskip_ref[:, cols] * xcf[:, cols])
                     * _silu(z_ref[r0:r0 + CHUNK, cols].astype(F32)))
            ybuf[:, cols] = gated
            ss = ss + jnp.sum(gated * gated, axis=-1, keepdims=True)
        inv = lax.rsqrt(ss * (1.0 / SSD_WIDTH) + EPS)
        for s0 in range(0, SSD_WIDTH, slab):
            cols = slice(s0, s0 + slab)
            y_ref[r0:r0 + CHUNK, cols] = (ybuf[:, cols] * inv * ng_ref[:, cols]).astype(y_ref.dtype)


def _shift_matrices():
    m = np.zeros((CONV_K - 1, CHUNK, 2 * CHUNK), np.float32)
    t = np.arange(CHUNK)
    for sh in range(1, CONV_K):
        m[sh - 1, t, CHUNK + t - sh] = 1.0
    return jnp.asarray(m, BF16)


def _ssd(proj, small, conv_w, conv_b, bias_row, alog_row, dskip_row, norm_g, tril, shifts, bsz, seq):
    t = proj.shape[0]
    tb = min(TB_SSD, seq)
    per_b = seq // tb
    row = lambda b, j: b * per_b + j
    full = lambda shape: pl.BlockSpec(shape, lambda b, j: (0,) * len(shape))
    return pl.pallas_call(
        functools.partial(_ssd_kernel, tb=tb),
        grid=(bsz, per_b),
        in_specs=[pl.BlockSpec((tb, SSD_WIDTH), lambda b, j: (row(b, j), COL_XS // SSD_WIDTH)),
                  pl.BlockSpec((tb, SSD_WIDTH), lambda b, j: (row(b, j), COL_Z // SSD_WIDTH)),
                  pl.BlockSpec((tb, BC_WIDTH), lambda b, j: (row(b, j), COL_B // BC_WIDTH)),
                  pl.BlockSpec((tb, BC_WIDTH), lambda b, j: (row(b, j), COL_C // BC_WIDTH)),
                  pl.BlockSpec((tb, SMALL_W), lambda b, j: (row(b, j), 0)),
                  full((CONV_K, CONV_DIM)), full((1, CONV_DIM)), full((1, SMALL_W)),
                  full((1, SMALL_W)), full((1, SSD_WIDTH)), full((1, SSD_WIDTH)),
                  full((CHUNK, CHUNK)), full((CONV_K - 1, CHUNK, 2 * CHUNK))],
        out_specs=[pl.BlockSpec((tb, SSD_WIDTH), lambda b, j: (row(b, j), 0)),
                   pl.BlockSpec((tb, SMALL_W), lambda b, j: (row(b, j), 0))],
        out_shape=[jax.ShapeDtypeStruct((t, SSD_WIDTH), BF16),
                   jax.ShapeDtypeStruct((t, SMALL_W), BF16)],
        scratch_shapes=[pltpu.VMEM((CHUNK + tb, CONV_DIM), BF16),
                        pltpu.VMEM((SSD_GROUPS, SSD_STATE, SSD_HEADS_PER_GROUP * SSD_HEAD_DIM), F32),
                        pltpu.VMEM((2, CHUNK, SSD_WIDTH), F32),
                        pltpu.VMEM((1, SMALL_W), F32),
                        pltpu.VMEM((2, CHUNK, CONV_DIM), F32),
                        pltpu.VMEM((2, CHUNK, SSD_WIDTH), BF16),
                        pltpu.VMEM((2, CHUNK, SSD_HEADS_PER_GROUP * SSD_HEAD_DIM), F32)],
        compiler_params=_cparams(("parallel", "arbitrary")),
        name="ssd",
    )(proj, proj, proj, proj, small, conv_w, conv_b, bias_row, alog_row, dskip_row, norm_g, tril,
      shifts)


def _fox_bounds_kernel(q_ref, k_ref, nf_ref, nft_ref, o_ref, *, tq, nq):
    pair = pl.program_id(1)
    rows = 16
    hrow = lax.broadcasted_iota(jnp.int32, (rows, LANES), 0)
    hlane = lax.broadcasted_iota(jnp.int32, (rows, LANES), 1)
    selector = jnp.where((hrow < 2) & ((hlane >= FOX_HEAD_DIM) == (hrow == 1)), 1.0, 0.0).astype(BF16)
    q = q_ref[...]
    k = k_ref[...]
    qn2 = _dot_nt(selector, q * q)
    kn2 = jnp.max(_dot_nt(selector, k * k), axis=1, keepdims=True)
    diag = _dot_nt(selector, q * k)
    qmax2 = jnp.zeros((rows, LANES), F32)
    dmin = jnp.zeros((rows, LANES), F32)
    for t in range(nq):
        tile = slice(t * tq, (t + 1) * tq)
        qmax2 = jnp.where(hlane == t, jnp.max(qn2[:, tile], axis=1, keepdims=True), qmax2)
        dmin = jnp.where(hlane == t, jnp.min(diag[:, tile], axis=1, keepdims=True), dmin)
    qk = jnp.sqrt(qmax2 * kn2) * NORM_SLACK

    nft = nft_ref[0]
    nft_prev = pltpu.roll(nft, 1, 1)
    nf = nf_ref[0]
    sub8 = lax.broadcasted_iota(jnp.int32, (FOX_HEADS, LANES), 0)
    lane8 = lax.broadcasted_iota(jnp.int32, (nq, FOX_HEADS), 1)
    kb_idx = lax.broadcasted_iota(jnp.int32, (nq, LANES), 0)
    t_idx = lax.broadcasted_iota(jnp.int32, (nq, LANES), 1)
    first = None
    for h in range(2):
        hd = 2 * pair + h
        prev_row = jnp.sum(jnp.where(sub8 == hd, nft_prev, 0.0), axis=0, keepdims=True)
        nf_col = jnp.sum(jnp.where(lane8 == hd, nf, 0.0), axis=1, keepdims=True)
        thresh = prev_row + dmin[h:h + 1, :] - NORM_SLACK * qk[h:h + 1, :] - PRUNE_MARGIN
        dead = (nf_col <= thresh) & (kb_idx < t_idx)
        count = jnp.sum(jnp.where(dead, 1.0, 0.0), axis=0, keepdims=True)
        first = count if first is None else jnp.minimum(first, count)
    o_ref[0, 0] = jnp.broadcast_to(first, (FOX_HEADS, LANES)).astype(jnp.int32)


def _fox_bounds(proj, f_aug, bsz, seq, tq):
    nq = seq // tq
    base = COL_FOX // LANES
    nblk = FOX_WIDTH // LANES
    ends = f_aug.reshape(bsz, nq, tq, SMALL_W)[:, :, tq - 1, 0:3 * FOX_HEADS].astype(F32)
    nf = ends.reshape(bsz, nq, 3, FOX_HEADS).sum(axis=2)
    nft = jnp.full((bsz, FOX_HEADS, LANES), jnp.inf, F32).at[:, :, 0:nq].set(nf.transpose(0, 2, 1))
    out = pl.pallas_call(
        functools.partial(_fox_bounds_kernel, tq=tq, nq=nq),
        grid=(bsz, FOX_PAIRS),
        in_specs=[pl.BlockSpec((seq, LANES), lambda b, p: (b, base + p)),
                  pl.BlockSpec((seq, LANES), lambda b, p: (b, base + nblk + p)),
                  pl.BlockSpec((1, nq, FOX_HEADS), lambda b, p: (b, 0, 0)),
                  pl.BlockSpec((1, FOX_HEADS, LANES), lambda b, p: (b, 0, 0))],
        out_specs=pl.BlockSpec((1, 1, FOX_HEADS, LANES), lambda b, p: (b, p, 0, 0)),
        out_shape=jax.ShapeDtypeStruct((bsz, FOX_PAIRS, FOX_HEADS, LANES), jnp.int32),
        compiler_params=_cparams(("parallel", "parallel")),
        name="fox_bounds",
    )(proj, proj, nf, nft)
    return out[:, :, 0, 0:nq].reshape(-1)


def _fox_kernel(start_ref, q_ref, k_ref, v_ref, g_ref, f_ref, o_ref, kaug, vaug, m_ref, acc_ref,
                *, tq, nq):
    bat = pl.program_id(0)
    pair = pl.program_id(1)
    i = pl.program_id(2)
    lane = lax.broadcasted_iota(jnp.int32, (1, LANES), 1)
    head0 = lane < FOX_HEAD_DIM

    @pl.when(i == 0)
    def _():
        kaug[:, 0:LANES] = k_ref[...]
        kaug[:, LANES:2 * LANES] = f_ref[...]
        vaug[:, 0:LANES] = v_ref[...]
        vaug[:, LANES:2 * LANES] = jnp.ones((vaug.shape[0], LANES), BF16)

    q = q_ref[...]
    zero = jnp.zeros_like(q)
    qh = []
    for h in range(2):
        hd = 2 * pair + h
        sel = (lane == hd) | (lane == hd + FOX_HEADS) | (lane == hd + 2 * FOX_HEADS)
        ones = jnp.broadcast_to(jnp.where(sel, 1.0, 0.0).astype(BF16), (tq, LANES))
        qm = jnp.where(head0, q, zero) if h == 0 else jnp.where(head0, zero, q)
        qh.append(jnp.concatenate([qm, ones], axis=1))
    start = start_ref[(bat * FOX_PAIRS + pair) * nq + i]

    m_ref[...] = jnp.full(m_ref.shape, NEG_BIG, F32)
    acc_ref[...] = jnp.zeros_like(acc_ref)

    def block(kb, masked):
        cols = pl.ds(pl.multiple_of(kb * tq, tq), tq)
        k = kaug[cols, :]
        v = vaug[cols, :]
        if masked:
            ri = lax.broadcasted_iota(jnp.int32, (tq, tq), 0)
            ci = lax.broadcasted_iota(jnp.int32, (tq, tq), 1)
            keep = ri >= ci
        for h in range(2):
            s = _dot_nt(qh[h], k)
            if masked:
                s = jnp.where(keep, s, NEG_BIG)
            m_old = m_ref[h]
            m_new = jnp.maximum(m_old, jnp.max(s, axis=-1, keepdims=True))
            alpha = jnp.exp2(m_old - m_new)
            p = jnp.exp2(s - jnp.concatenate([m_new] * (tq // LANES), axis=1))
            acc_ref[h] = (jnp.concatenate([alpha, alpha], axis=1) * acc_ref[h]
                          + _dot(p.astype(BF16), v))
            m_ref[h] = m_new

    def body(kq, carry):
        for u in range(KV_UNROLL):
            block(start + KV_UNROLL * kq + u, False)
        return carry

    full = (i - start) // KV_UNROLL
    lax.fori_loop(0, full, body, 0)
    tail0 = start + full * KV_UNROLL
    for extra in range(KV_UNROLL):
        @pl.when(i - tail0 == extra)
        def _(extra=extra):
            for u in range(extra):
                block(tail0 + u, False)
            block(i, True)

    o = jnp.where(head0, acc_ref[0, :, 0:LANES] / acc_ref[0, :, LANES:2 * LANES],
                  acc_ref[1, :, 0:LANES] / acc_ref[1, :, LANES:2 * LANES])
    o = o * _silu(g_ref[...].astype(F32))
    o_ref[...] = o.astype(o_ref.dtype)


def _fox(proj, f_aug, bsz, seq):
    t = proj.shape[0]
    tq = min(T_ATT, seq)
    nq = seq // tq
    base = COL_FOX // LANES
    nblk = FOX_WIDTH // LANES
    starts = _fox_bounds(proj, f_aug, bsz, seq, tq)
    return pl.pallas_call(
        functools.partial(_fox_kernel, tq=tq, nq=nq),
        grid=(bsz, FOX_PAIRS, nq),
        in_specs=[pl.BlockSpec(memory_space=pltpu.SMEM),
                  pl.BlockSpec((tq, LANES), lambda b, p, i: (b * nq + i, base + p)),
                  pl.BlockSpec((seq, LANES), lambda b, p, i: (b, base + nblk + p)),
                  pl.BlockSpec((seq, LANES), lambda b, p, i: (b, base + 2 * nblk + p)),
                  pl.BlockSpec((tq, LANES), lambda b, p, i: (b * nq + i, base + 3 * nblk + p)),
                  pl.BlockSpec((seq, SMALL_W), lambda b, p, i: (b, 0))],
        out_specs=pl.BlockSpec((tq, LANES), lambda b, p, i: (b * nq + i, p)),
        out_shape=jax.ShapeDtypeStruct((t, FOX_WIDTH), BF16),
        scratch_shapes=[pltpu.VMEM((seq, 2 * LANES), BF16),
                        pltpu.VMEM((seq, 2 * LANES), BF16),
                        pltpu.VMEM((2, tq, LANES), F32),
                        pltpu.VMEM((2, tq, 2 * LANES), F32)],
        compiler_params=_cparams(("parallel", "parallel", "arbitrary")),
        name="fox_attention",
    )(starts, proj, proj, proj, proj, f_aug)


def _out_kernel(ret_ref, ssd_ref, fox_ref, w_ref, x_ref, gate_ref, g_ref, sc_ref, sh_ref,
                *out_refs, emit_x):
    acc = _dot(ret_ref[...], w_ref[0:RET_WIDTH, :])
    acc = acc + _dot(ssd_ref[...], w_ref[RET_WIDTH:RET_WIDTH + SSD_WIDTH, :])
    acc = acc + _dot(fox_ref[...], w_ref[RET_WIDTH + SSD_WIDTH:MIX_WIDTH, :])
    xn = x_ref[...] + gate_ref[0] * acc
    if emit_x:
        out_refs[0][...] = xn
    y_ref = out_refs[-1]
    y_ref[...] = _modnorm(xn, g_ref[...], sc_ref[0], sh_ref[0]).astype(y_ref.dtype)


def _out_projection(ret, ssd, fox, w_out, layer, x2, gate, g, scale, shift, seq, y_dtype, emit_x):
    t, d = x2.shape
    tm = min(TM_OUT, seq)
    per_b = seq // tm
    rowblk = lambda w: pl.BlockSpec((tm, w), lambda i: (i, 0))
    perb = pl.BlockSpec((1, 1, d), lambda i: (i // per_b, 0, 0))
    out_specs = [rowblk(d)]
    out_shape = [jax.ShapeDtypeStruct((t, d), y_dtype)]
    if emit_x:
        out_specs = [rowblk(d)] + out_specs
        out_shape = [jax.ShapeDtypeStruct((t, d), F32)] + out_shape
    return pl.pallas_call(
        functools.partial(_out_kernel, emit_x=emit_x),
        grid=(t // tm,),
        in_specs=[rowblk(RET_WIDTH), rowblk(SSD_WIDTH), rowblk(FOX_WIDTH),
                  pl.BlockSpec((None, MIX_WIDTH, d), lambda i: (layer, 0, 0)),
                  rowblk(d), perb, pl.BlockSpec((1, d), lambda i: (0, 0)), perb, perb],
        out_specs=out_specs,
        out_shape=out_shape,
        compiler_params=_cparams(("parallel",)),
        name="out_proj",
    )(ret, ssd, fox, w_out, x2, gate, g.reshape(1, d), scale, shift)


def _permute_in_weights(w_in):
    o_xbc = 4 * RET_WIDTH
    o_dt = o_xbc + CONV_DIM
    o_z = o_dt + SSD_HEADS
    o_fox = o_z + SSD_WIDTH
    o_f = o_fox + 4 * FOX_WIDTH
    col_scale = np.ones((w_in.shape[-1],), np.float32)
    col_scale[o_fox:o_fox + FOX_WIDTH] = FOX_Q_SCALE
    wb = (w_in * col_scale).astype(BF16)
    main = jnp.concatenate([
        wb[..., 0:o_xbc + SSD_WIDTH],
        wb[..., o_z:o_z + SSD_WIDTH],
        wb[..., o_xbc + SSD_WIDTH:o_dt],
        wb[..., o_fox:o_f],
    ], axis=-1)
    pad = jnp.zeros(wb.shape[:-1] + (SMALL_W - FOX_HEADS - SSD_HEADS,), BF16)
    small = jnp.concatenate([wb[..., o_f:o_f + FOX_HEADS], wb[..., o_dt:o_z], pad], axis=-1)
    return main, small


def _lane_row(depth, pieces):
    row = jnp.zeros((depth, 1, SMALL_W), F32)
    for start, arr in pieces:
        row = row.at[:, 0, start:start + arr.shape[-1]].set(arr.astype(F32))
    return row


def kernel(x, c, positions, norm_g, w_ada, b_ada, w_in, conv_w, conv_b, dt_bias, a_log, d_skip,
           ssd_norm_g, b_forget, w_out, final_g):
    bsz, seq, d = x.shape
    depth = w_in.shape[0]
    t = bsz * seq

    mod = _modulation(c, w_ada, b_ada)
    shift = mod[:, :, 0:d].reshape(depth, bsz, 1, d)
    scale = mod[:, :, d:2 * d].reshape(depth, bsz, 1, d)
    gate = mod[:, :, 2 * d:3 * d].reshape(depth, bsz, 1, d)

    cos_t, sin_t = _rope_tables(positions)
    w_main, w_small = _permute_in_weights(w_in)
    w_out_b = w_out.astype(BF16)
    ret_consts = _retention_consts()
    tril = jnp.asarray(np.tril(np.ones((CHUNK, CHUNK), np.float32)), BF16)
    shifts = _shift_matrices()
    bias_rows = _lane_row(depth, [(LANE_F, b_forget), (LANE_DT, dt_bias)])
    alog_rows = _lane_row(depth, [(LANE_DT, a_log)])
    dskip_rows = jnp.repeat(d_skip, SSD_HEAD_DIM, axis=-1).reshape(depth, 1, SSD_WIDTH)
    zeros_bd = jnp.zeros((bsz, 1, d), F32)

    x2 = x.reshape(t, d)
    h = _first_norm(x2, norm_g[0], scale[0], shift[0], seq)
    out = None
    for l in range(depth):
        proj, small = _in_projection(h, w_main, w_small, l)
        ret = _retention(proj, cos_t, sin_t, ret_consts, bsz, seq)
        ssd, f_rows = _ssd(proj, small, conv_w[l], conv_b[l].reshape(1, CONV_DIM), bias_rows[l],
                           alog_rows[l], dskip_rows[l], ssd_norm_g[l].reshape(1, SSD_WIDTH), tril,
                           shifts, bsz, seq)
        fox = _fox(proj, f_rows, bsz, seq)
        if l + 1 < depth:
            x2, h = _out_projection(ret, ssd, fox, w_out_b, l, x2, gate[l], norm_g[l + 1],
                                    scale[l + 1], shift[l + 1], seq, BF16, True)
        else:
            (out,) = _out_projection(ret, ssd, fox, w_out_b, l, x2, gate[l], final_g,
                                     zeros_bd, zeros_bd, seq, F32, False)
    return out.reshape(bsz, seq, d)
```

```python
import functools
import math

import numpy as np
import jax
import jax.numpy as jnp
from jax import lax
from jax.experimental import pallas as pl
from jax.experimental.pallas import tpu as pltpu

F32 = jnp.float32
BF16 = jnp.bfloat16

RET_HEADS = 4
RET_HEAD_DIM = 128
RET_WIDTH = RET_HEADS * RET_HEAD_DIM
SSD_HEADS = 16
SSD_HEAD_DIM = 64
SSD_WIDTH = SSD_HEADS * SSD_HEAD_DIM
SSD_GROUPS = 2
SSD_STATE = 128
SSD_HEADS_PER_GROUP = SSD_HEADS // SSD_GROUPS
CONV_K = 4
BC_WIDTH = SSD_GROUPS * SSD_STATE
CONV_DIM = SSD_WIDTH + 2 * BC_WIDTH
FOX_HEADS = 8
FOX_HEAD_DIM = 64
FOX_WIDTH = FOX_HEADS * FOX_HEAD_DIM
FOX_PAIRS = FOX_HEADS // 2
MIX_WIDTH = RET_WIDTH + SSD_WIDTH + FOX_WIDTH
CHUNK = 128
ROPE_BASE = 10000.0
EPS = 1e-6
LOG2E = math.log2(math.e)
FOX_Q_SCALE = FOX_HEAD_DIM ** -0.5 * LOG2E

COL_RET = 0
COL_XS = 4 * RET_WIDTH
COL_Z = COL_XS + SSD_WIDTH
COL_B = COL_Z + SSD_WIDTH
COL_C = COL_B + BC_WIDTH
COL_FOX = COL_C + BC_WIDTH
N_MAIN = COL_FOX + 4 * FOX_WIDTH
SMALL_W = 128
LANE_F = 0
LANE_DT = FOX_HEADS

LANES = 128
TM_PROJ = 1024
TN_PROJ = 3328
TM_OUT = 512
TM_NORM = 1024
TB_RET = 1024
TB_SSD = 1024
T_ATT = 512
KV_UNROLL = 4
PRUNE_MARGIN = 152.0
NORM_SLACK = 1.02
VMEM_LIMIT = 48 * 1024 * 1024
NEG_BIG = -1e30


def _cparams(sem):
    return pltpu.CompilerParams(dimension_semantics=sem, vmem_limit_bytes=VMEM_LIMIT)


def _dot(a, b):
    return jnp.dot(a, b, preferred_element_type=F32)


def _dot_nt(a, b):
    return lax.dot_general(a, b, (((1,), (1,)), ((), ())), preferred_element_type=F32)


def _dot_tn(a, b):
    return lax.dot_general(a, b, (((0,), (0,)), ((), ())), preferred_element_type=F32)


def _silu(t):
    return t * jax.nn.sigmoid(t)


def _split3(t):
    hi = t.astype(BF16)
    r1 = t - hi.astype(F32)
    mid = r1.astype(BF16)
    lo = (r1 - mid.astype(F32)).astype(BF16)
    return hi, mid, lo


def _modnorm(t, g, scale, shift):
    var = jnp.mean(t * t, axis=-1, keepdims=True)
    return t * lax.rsqrt(var + EPS) * g * (1.0 + scale) + shift


def _mod_kernel(c_ref, w_ref, b_ref, o_ref):
    a = _silu(c_ref[...])
    w = w_ref[0]
    a_hi = a.astype(BF16)
    a_lo = (a - a_hi.astype(F32)).astype(BF16)
    w_hi = w.astype(BF16)
    w_lo = (w - w_hi.astype(F32)).astype(BF16)
    o_ref[0] = _dot(a_hi, w_hi) + _dot(a_hi, w_lo) + _dot(a_lo, w_hi) + b_ref[0]


def _modulation(c, w_ada, b_ada):
    depth, d, n3 = w_ada.shape
    bsz = c.shape[0]
    rows = 16
    c_pad = jnp.zeros((rows, d), F32).at[:bsz].set(c)
    tn = 1024
    out = pl.pallas_call(
        _mod_kernel,
        grid=(depth, n3 // tn),
        in_specs=[pl.BlockSpec((rows, d), lambda l, j: (0, 0)),
                  pl.BlockSpec((1, d, tn), lambda l, j: (l, 0, j)),
                  pl.BlockSpec((1, 1, tn), lambda l, j: (l, 0, j))],
        out_specs=pl.BlockSpec((1, rows, tn), lambda l, j: (l, 0, j)),
        out_shape=jax.ShapeDtypeStruct((depth, rows, n3), F32),
        compiler_params=_cparams(("parallel", "parallel")),
        name="adaln_mod",
    )(c_pad, w_ada, b_ada.reshape(depth, 1, n3))
    return out[:, :bsz]


def _rope_kernel(pos_ref, freq_ref, cos_ref, sin_ref):
    pos = pos_ref[0].astype(F32)
    ang = freq_ref[...] * pos
    c = jnp.cos(ang)
    s = jnp.sin(ang)
    cos_ref[...] = jnp.concatenate([c, c], axis=0).T
    sin_ref[...] = jnp.concatenate([-s, s], axis=0).T


def _rope_tables(positions):
    bsz, seq = positions.shape
    half = RET_HEAD_DIM // 2
    freq = (ROPE_BASE ** (-jnp.arange(half, dtype=F32) / half)).reshape(half, 1)
    tm = 512
    nt = seq // tm
    shp = jax.ShapeDtypeStruct((bsz * seq, RET_HEAD_DIM), F32)
    return pl.pallas_call(
        _rope_kernel,
        grid=(bsz, nt),
        in_specs=[pl.BlockSpec((1, 1, tm), lambda b, i: (b, 0, i)),
                  pl.BlockSpec((half, 1), lambda b, i: (0, 0))],
        out_specs=[pl.BlockSpec((tm, RET_HEAD_DIM), lambda b, i: (b * nt + i, 0)),
                   pl.BlockSpec((tm, RET_HEAD_DIM), lambda b, i: (b * nt + i, 0))],
        out_shape=[shp, shp],
        compiler_params=_cparams(("parallel", "parallel")),
        name="rope_tables",
    )(positions.reshape(bsz, 1, seq), freq)


def _norm_kernel(x_ref, g_ref, sc_ref, sh_ref, h_ref):
    h_ref[...] = _modnorm(x_ref[...], g_ref[...], sc_ref[0], sh_ref[0]).astype(h_ref.dtype)


def _first_norm(x2, g, scale, shift, seq):
    t, d = x2.shape
    tm = min(TM_NORM, seq)
    per_b = seq // tm
    return pl.pallas_call(
        _norm_kernel,
        grid=(t // tm,),
        in_specs=[pl.BlockSpec((tm, d), lambda i: (i, 0)),
                  pl.BlockSpec((1, d), lambda i: (0, 0)),
                  pl.BlockSpec((1, 1, d), lambda i: (i // per_b, 0, 0)),
                  pl.BlockSpec((1, 1, d), lambda i: (i // per_b, 0, 0))],
        out_specs=pl.BlockSpec((tm, d), lambda i: (i, 0)),
        out_shape=jax.ShapeDtypeStruct((t, d), BF16),
        compiler_params=_cparams(("parallel",)),
        name="first_norm",
    )(x2, g.reshape(1, d), scale, shift)


def _inproj_kernel(h_ref, w_ref, ws_ref, o_ref, os_ref):
    h = h_ref[...]
    o_ref[...] = _dot(h, w_ref[...]).astype(o_ref.dtype)

    @pl.when(pl.program_id(1) == 0)
    def _():
        os_ref[...] = _dot(h, ws_ref[...])


def _in_projection(h, w_main, w_small, layer):
    t, d = h.shape
    tm = min(TM_PROJ, t)
    return pl.pallas_call(
        _inproj_kernel,
        grid=(t // tm, N_MAIN // TN_PROJ),
        in_specs=[pl.BlockSpec((tm, d), lambda i, j: (i, 0)),
                  pl.BlockSpec((None, d, TN_PROJ), lambda i, j: (layer, 0, j)),
                  pl.BlockSpec((None, d, SMALL_W), lambda i, j: (layer, 0, 0))],
        out_specs=[pl.BlockSpec((tm, TN_PROJ), lambda i, j: (i, j)),
                   pl.BlockSpec((tm, SMALL_W), lambda i, j: (i, 0))],
        out_shape=[jax.ShapeDtypeStruct((t, N_MAIN), BF16),
                   jax.ShapeDtypeStruct((t, SMALL_W), F32)],
        compiler_params=_cparams(("parallel", "arbitrary")),
        name="in_proj",
    )(h, w_main, w_small)


def _retention_consts():
    h = np.arange(RET_HEADS, dtype=np.float64)
    log_g = np.log(1.0 - 2.0 ** (-5.0 - h))
    idx = np.arange(CHUNK, dtype=np.float64)
    diff = idx[:, None] - idx[None, :]
    intra = np.where(diff >= 0, np.exp(log_g[:, None, None] * np.maximum(diff, 0.0)), 0.0)
    dq = np.exp(log_g[:, None] * (idx + 1.0))
    dk = np.exp(log_g[:, None] * (CHUNK - 1.0 - idx))
    dq = np.broadcast_to(dq[:, :, None], (RET_HEADS, CHUNK, RET_HEAD_DIM))
    dk = np.broadcast_to(dk[:, :, None], (RET_HEADS, CHUNK, RET_HEAD_DIM))
    dchunk = tuple(float(v) for v in np.exp(log_g * CHUNK))
    return (jnp.asarray(intra, F32), jnp.asarray(dq, F32), jnp.asarray(dk, F32), dchunk)


def _ret_kernel(q_ref, k_ref, v_ref, g_ref, cos_ref, sin_ref, di_ref, dq_ref, dk_ref,
                o_ref, s_ref, *, nchunk, dchunk):
    @pl.when(pl.program_id(1) == 0)
    def _():
        s_ref[...] = jnp.zeros_like(s_ref)

    kscale = RET_HEAD_DIM ** -0.5

    def body(c, carry):
        rows = pl.ds(pl.multiple_of(c * CHUNK, CHUNK), CHUNK)
        cs = cos_ref[rows, :]
        sn = sin_ref[rows, :]
        for h in range(RET_HEADS):
            cols = slice(h * RET_HEAD_DIM, (h + 1) * RET_HEAD_DIM)
            q = q_ref[rows, cols].astype(F32)
            k = k_ref[rows, cols].astype(F32)
            v = v_ref[rows, cols]
            qr = q * cs + pltpu.roll(q, RET_HEAD_DIM // 2, 1) * sn
            kr = (k * cs + pltpu.roll(k, RET_HEAD_DIM // 2, 1) * sn) * kscale
            qb = qr.astype(BF16)
            kb = kr.astype(BF16)
            state = s_ref[h]
            s = _dot_nt(qb, kb) * di_ref[h]
            o = _dot(s.astype(BF16), v) + _dot(qb, state.astype(BF16)) * dq_ref[h]
            kd = (kr * dk_ref[h]).astype(BF16)
            s_ref[h] = state * dchunk[h] + _dot_tn(kd, v)
            o = o * lax.rsqrt(jnp.mean(o * o, axis=-1, keepdims=True) + EPS)
            o = o * _silu(g_ref[rows, cols].astype(F32))
            o_ref[rows, cols] = o.astype(o_ref.dtype)
        return carry

    lax.fori_loop(0, nchunk, body, 0, unroll=4)


def _retention(proj, cos_t, sin_t, consts, bsz, seq):
    t = proj.shape[0]
    tb = min(TB_RET, seq)
    per_b = seq // tb
    intra, dq, dk, dchunk = consts
    row = lambda b, j: b * per_b + j
    cblk = lambda idx: pl.BlockSpec((tb, RET_WIDTH), lambda b, j: (row(b, j), idx))
    const3 = pl.BlockSpec((RET_HEADS, CHUNK, RET_HEAD_DIM), lambda b, j: (0, 0, 0))
    tblk = pl.BlockSpec((tb, RET_HEAD_DIM), lambda b, j: (row(b, j), 0))
    base = COL_RET // RET_WIDTH
    return pl.pallas_call(
        functools.partial(_ret_kernel, nchunk=tb // CHUNK, dchunk=dchunk),
        grid=(bsz, per_b),
        in_specs=[cblk(base), cblk(base + 1), cblk(base + 2), cblk(base + 3),
                  tblk, tblk, const3, const3, const3],
        out_specs=pl.BlockSpec((tb, RET_WIDTH), lambda b, j: (row(b, j), 0)),
        out_shape=jax.ShapeDtypeStruct((t, RET_WIDTH), BF16),
        scratch_shapes=[pltpu.VMEM((RET_HEADS, RET_HEAD_DIM, RET_HEAD_DIM), F32)],
        compiler_params=_cparams(("parallel", "arbitrary")),
        name="retention",
    )(proj, proj, proj, proj, cos_t, sin_t, intra, dq, dk)


def _ssd_kernel(xs_ref, z_ref, b_ref, c_ref, sm_ref, cw_ref, cb_ref, bias_ref, alog_ref,
                dskip_ref, ng_ref, tril_ref, shift_ref, y_ref, f_ref,
                ubuf, st_ref, ybuf2, fcar, xcf2, xsb2, inter2, *, tb):
    j = pl.program_id(1)

    @pl.when(j == 0)
    def _():
        ubuf[0:CHUNK, :] = jnp.zeros((CHUNK, CONV_DIM), BF16)
        st_ref[...] = jnp.zeros_like(st_ref)
        fcar[...] = jnp.zeros_like(fcar)

    @pl.when(j > 0)
    def _():
        ubuf[0:CHUNK, :] = ubuf[tb:tb + CHUNK, :]

    ubuf[CHUNK:CHUNK + tb, 0:SSD_WIDTH] = xs_ref[...]
    ubuf[CHUNK:CHUNK + tb, SSD_WIDTH:SSD_WIDTH + BC_WIDTH] = b_ref[...]
    ubuf[CHUNK:CHUNK + tb, SSD_WIDTH + BC_WIDTH:CONV_DIM] = c_ref[...]

    slab = 2 * LANES

    def conv_silu(c, xcf, xsb):
        for s0 in range(0, CONV_DIM, slab):
            cols = slice(s0, s0 + slab)
            win = ubuf[c * CHUNK:(c + 2) * CHUNK, cols]
            acc = cb_ref[:, cols] + cw_ref[CONV_K - 1:CONV_K, cols] * win[CHUNK:2 * CHUNK, :].astype(F32)
            for sh in range(1, CONV_K):
                tap = CONV_K - 1 - sh
                acc = acc + cw_ref[tap:tap + 1, cols] * _dot(shift_ref[sh - 1], win)
            xc = _silu(acc)
            xcf[:, cols] = xc
            if s0 < SSD_WIDTH:
                xsb[:, cols] = xc.astype(BF16)

    lane = lax.broadcasted_iota(jnp.int32, (1, LANES), 1)
    is_f = lane < LANE_DT
    half_lane = lane < SSD_HEAD_DIM
    a_row = jnp.where((lane >= LANE_DT) & (lane < LANE_DT + SSD_HEADS), -jnp.exp(alog_ref[...]), 0.0)
    ri = lax.broadcasted_iota(jnp.int32, (CHUNK, CHUNK), 0)
    ci = lax.broadcasted_iota(jnp.int32, (CHUNK, CHUNK), 1)
    causal = ri >= ci
    tril = tril_ref[...]

    for c in range(tb // CHUNK):
        r0 = c * CHUNK
        xcf, xsb, ybuf, inter = xcf2.at[c % 2], xsb2.at[c % 2], ybuf2.at[c % 2], inter2.at[c % 2]
        conv_silu(c, xcf, xsb)

        pre = sm_ref[r0:r0 + CHUNK, :] + bias_ref[...]
        tail_term = jnp.log1p(jnp.exp(-jnp.abs(pre)))
        sp = jnp.maximum(pre, 0.0) + tail_term
        val = jnp.where(is_f, -(jnp.maximum(-pre, 0.0) + tail_term), sp * a_row)
        v_hi, v_mid, v_lo = _split3(val)
        cum = _dot(tril, v_hi) + _dot(tril, v_mid) + _dot(tril, v_lo)
        cum = cum + fcar[...]
        fcar[...] = jnp.where(is_f, cum[CHUNK - 1:CHUNK, :], 0.0)
        cum_t = cum.T
        dt_t = sp.T
        n_hi, n_mid, n_lo = _split3(jnp.where(is_f, cum * -LOG2E, 0.0))
        f_ref[r0:r0 + CHUNK, :] = (n_hi.astype(F32) + pltpu.roll(n_mid.astype(F32), FOX_HEADS, 1)
                                   + pltpu.roll(n_lo.astype(F32), 2 * FOX_HEADS, 1)).astype(BF16)
        e_col = jnp.exp(cum)

        for g in range(SSD_GROUPS):
            bcol = SSD_WIDTH + g * SSD_STATE
            bm = xcf[:, bcol:bcol + SSD_STATE]
            cm_b = xcf[:, bcol + BC_WIDTH:bcol + BC_WIDTH + SSD_STATE].astype(BF16)
            cb = _dot_nt(cm_b, bm.astype(BF16))
            bm_t = bm.T
            inter[...] = _dot(cm_b, st_ref[g].astype(BF16))
            for pr in range(SSD_HEADS_PER_GROUP // 2):
                gp = g * (SSD_HEADS_PER_GROUP // 2) + pr
                xp_b = xsb[:, gp * LANES:(gp + 1) * LANES]
                ys = []
                ups = []
                ecs = []
                els = []
                for hh in range(2):
                    ln = LANE_DT + 2 * gp + hh
                    a_c = cum[:, ln:ln + 1]
                    a_r = cum_t[ln:ln + 1, :]
                    d_r = dt_t[ln:ln + 1, :]
                    lm = jnp.where(causal, jnp.exp(a_c - a_r), 0.0)
                    m = cb * lm * d_r
                    ys.append(_dot(m.astype(BF16), xp_b))
                    last = a_r[:, CHUNK - 1:CHUNK]
                    w_r = d_r * jnp.exp(last - a_r)
                    ups.append(_dot((bm_t * w_r).astype(BF16), xp_b))
                    ecs.append(e_col[:, ln:ln + 1])
                    els.append(jnp.exp(last))
                lanes = slice(pr * LANES, (pr + 1) * LANES)
                y_pair = (jnp.where(half_lane, ys[0], ys[1])
                          + jnp.where(half_lane, ecs[0], ecs[1]) * inter[:, lanes])
                ybuf[:, gp * LANES:(gp + 1) * LANES] = y_pair
                st_ref[g, :, lanes] = (st_ref[g, :, lanes] * jnp.where(half_lane, els[0], els[1])
                                       + jnp.where(half_lane, ups[0], ups[1]))

        ss = jnp.zeros((CHUNK, 1), F32)
        for s0 in range(0, SSD_WIDTH, slab):
            cols = slice(s0, s0 + slab)
            gated = ((ybuf[:, cols] + dskip_ref[:, cols] * xcf[:, cols])
                     * _silu(z_ref[r0:r0 + CHUNK, cols].astype(F32)))
            ybuf[:, cols] = gated
            ss = ss + jnp.sum(gated * gated, axis=-1, keepdims=True)
        inv = lax.rsqrt(ss * (1.0 / SSD_WIDTH) + EPS)
        for s0 in range(0, SSD_WIDTH, slab):
            cols = slice(s0, s0 + slab)
            y_ref[r0:r0 + CHUNK, cols] = (ybuf[:, cols] * inv * ng_ref[:, cols]).astype(y_ref.dtype)


def _shift_matrices():
    m = np.zeros((CONV_K - 1, CHUNK, 2 * CHUNK), np.float32)
    t = np.arange(CHUNK)
    for sh in range(1, CONV_K):
        m[sh - 1, t, CHUNK + t - sh] = 1.0
    return jnp.asarray(m, BF16)


def _ssd(proj, small, conv_w, conv_b, bias_row, alog_row, dskip_row, norm_g, tril, shifts, bsz, seq):
    t = proj.shape[0]
    tb = min(TB_SSD, seq)
    per_b = seq // tb
    row = lambda b, j: b * per_b + j
    full = lambda shape: pl.BlockSpec(shape, lambda b, j: (0,) * len(shape))
    return pl.pallas_call(
        functools.partial(_ssd_kernel, tb=tb),
        grid=(bsz, per_b),
        in_specs=[pl.BlockSpec((tb, SSD_WIDTH), lambda b, j: (row(b, j), COL_XS // SSD_WIDTH)),
                  pl.BlockSpec((tb, SSD_WIDTH), lambda b, j: (row(b, j), COL_Z // SSD_WIDTH)),
                  pl.BlockSpec((tb, BC_WIDTH), lambda b, j: (row(b, j), COL_B // BC_WIDTH)),
                  pl.BlockSpec((tb, BC_WIDTH), lambda b, j: (row(b, j), COL_C // BC_WIDTH)),
                  pl.BlockSpec((tb, SMALL_W), lambda b, j: (row(b, j), 0)),
                  full((CONV_K, CONV_DIM)), full((1, CONV_DIM)), full((1, SMALL_W)),
                  full((1, SMALL_W)), full((1, SSD_WIDTH)), full((1, SSD_WIDTH)),
                  full((CHUNK, CHUNK)), full((CONV_K - 1, CHUNK, 2 * CHUNK))],
        out_specs=[pl.BlockSpec((tb, SSD_WIDTH), lambda b, j: (row(b, j), 0)),
                   pl.BlockSpec((tb, SMALL_W), lambda b, j: (row(b, j), 0))],
        out_shape=[jax.ShapeDtypeStruct((t, SSD_WIDTH), BF16),
                   jax.ShapeDtypeStruct((t, SMALL_W), BF16)],
        scratch_shapes=[pltpu.VMEM((CHUNK + tb, CONV_DIM), BF16),
                        pltpu.VMEM((SSD_GROUPS, SSD_STATE, SSD_HEADS_PER_GROUP * SSD_HEAD_DIM), F32),
                        pltpu.VMEM((2, CHUNK, SSD_WIDTH), F32),
                        pltpu.VMEM((1, SMALL_W), F32),
                        pltpu.VMEM((2, CHUNK, CONV_DIM), F32),
                        pltpu.VMEM((2, CHUNK, SSD_WIDTH), BF16),
                        pltpu.VMEM((2, CHUNK, SSD_HEADS_PER_GROUP * SSD_HEAD_DIM), F32)],
        compiler_params=_cparams(("parallel", "arbitrary")),
        name="ssd",
    )(proj, proj, proj, proj, small, conv_w, conv_b, bias_row, alog_row, dskip_row, norm_g, tril,
      shifts)


def _fox_bounds_kernel(q_ref, k_ref, nf_ref, nft_ref, o_ref, *, tq, nq):
    pair = pl.program_id(1)
    rows = 16
    hrow = lax.broadcasted_iota(jnp.int32, (rows, LANES), 0)
    hlane = lax.broadcasted_iota(jnp.int32, (rows, LANES), 1)
    selector = jnp.where((hrow < 2) & ((hlane >= FOX_HEAD_DIM) == (hrow == 1)), 1.0, 0.0).astype(BF16)
    q = q_ref[...]
    k = k_ref[...]
    qn2 = _dot_nt(selector, q * q)
    kn2 = jnp.max(_dot_nt(selector, k * k), axis=1, keepdims=True)
    diag = _dot_nt(selector, q * k)
    qmax2 = jnp.zeros((rows, LANES), F32)
    dmin = jnp.zeros((rows, LANES), F32)
    for t in range(nq):
        tile = slice(t * tq, (t + 1) * tq)
        qmax2 = jnp.where(hlane == t, jnp.max(qn2[:, tile], axis=1, keepdims=True), qmax2)
        dmin = jnp.where(hlane == t, jnp.min(diag[:, tile], axis=1, keepdims=True), dmin)
    qk = jnp.sqrt(qmax2 * kn2) * NORM_SLACK

    nft = nft_ref[0]
    nft_prev = pltpu.roll(nft, 1, 1)
    nf = nf_ref[0]
    sub8 = lax.broadcasted_iota(jnp.int32, (FOX_HEADS, LANES), 0)
    lane8 = lax.broadcasted_iota(jnp.int32, (nq, FOX_HEADS), 1)
    kb_idx = lax.broadcasted_iota(jnp.int32, (nq, LANES), 0)
    t_idx = lax.broadcasted_iota(jnp.int32, (nq, LANES), 1)
    first = None
    for h in range(2):
        hd = 2 * pair + h
        prev_row = jnp.sum(jnp.where(sub8 == hd, nft_prev, 0.0), axis=0, keepdims=True)
        nf_col = jnp.sum(jnp.where(lane8 == hd, nf, 0.0), axis=1, keepdims=True)
        thresh = prev_row + dmin[h:h + 1, :] - NORM_SLACK * qk[h:h + 1, :] - PRUNE_MARGIN
        dead = (nf_col <= thresh) & (kb_idx < t_idx)
        count = jnp.sum(jnp.where(dead, 1.0, 0.0), axis=0, keepdims=True)
        first = count if first is None else jnp.minimum(first, count)
    o_ref[0, 0] = jnp.broadcast_to(first, (FOX_HEADS, LANES)).astype(jnp.int32)


def _fox_bounds(proj, f_aug, bsz, seq, tq):
    nq = seq // tq
    base = COL_FOX // LANES
    nblk = FOX_WIDTH // LANES
    ends = f_aug.reshape(bsz, nq, tq, SMALL_W)[:, :, tq - 1, 0:3 * FOX_HEADS].astype(F32)
    nf = ends.reshape(bsz, nq, 3, FOX_HEADS).sum(axis=2)
    nft = jnp.full((bsz, FOX_HEADS, LANES), jnp.inf, F32).at[:, :, 0:nq].set(nf.transpose(0, 2, 1))
    out = pl.pallas_call(
        functools.partial(_fox_bounds_kernel, tq=tq, nq=nq),
        grid=(bsz, FOX_PAIRS),
        in_specs=[pl.BlockSpec((seq, LANES), lambda b, p: (b, base + p)),
                  pl.BlockSpec((seq, LANES), lambda b, p: (b, base + nblk + p)),
                  pl.BlockSpec((1, nq, FOX_HEADS), lambda b, p: (b, 0, 0)),
                  pl.BlockSpec((1, FOX_HEADS, LANES), lambda b, p: (b, 0, 0))],
        out_specs=pl.BlockSpec((1, 1, FOX_HEADS, LANES), lambda b, p: (b, p, 0, 0)),
        out_shape=jax.ShapeDtypeStruct((bsz, FOX_PAIRS, FOX_HEADS, LANES), jnp.int32),
        compiler_params=_cparams(("parallel", "parallel")),
        name="fox_bounds",
    )(proj, proj, nf, nft)
    return out[:, :, 0, 0:nq].reshape(-1)


def _fox_kernel(start_ref, q_ref, k_ref, v_ref, g_ref, f_ref, o_ref, kaug, vaug, m_ref, acc_ref,
                *, tq, nq):
    bat = pl.program_id(0)
    pair = pl.program_id(1)
    i = pl.program_id(2)
    lane = lax.broadcasted_iota(jnp.int32, (1, LANES), 1)
    head0 = lane < FOX_HEAD_DIM

    @pl.when(i == 0)
    def _():
        kaug[:, 0:LANES] = k_ref[...]
        kaug[:, LANES:2 * LANES] = f_ref[...]
        vaug[:, 0:LANES] = v_ref[...]
        vaug[:, LANES:2 * LANES] = jnp.ones((vaug.shape[0], LANES), BF16)

    q = q_ref[...]
    zero = jnp.zeros_like(q)
    qh = []
    for h in range(2):
        hd = 2 * pair + h
        sel = (lane == hd) | (lane == hd + FOX_HEADS) | (lane == hd + 2 * FOX_HEADS)
        ones = jnp.broadcast_to(jnp.where(sel, 1.0, 0.0).astype(BF16), (tq, LANES))
        qm = jnp.where(head0, q, zero) if h == 0 else jnp.where(head0, zero, q)
        qh.append(jnp.concatenate([qm, ones], axis=1))
    start = start_ref[(bat * FOX_PAIRS + pair) * nq + i]

    m_ref[...] = jnp.full(m_ref.shape, NEG_BIG, F32)
    acc_ref[...] = jnp.zeros_like(acc_ref)

    def block(kb, masked):
        cols = pl.ds(pl.multiple_of(kb * tq, tq), tq)
        k = kaug[cols, :]
        v = vaug[cols, :]
        if masked:
            ri = lax.broadcasted_iota(jnp.int32, (tq, tq), 0)
            ci = lax.broadcasted_iota(jnp.int32, (tq, tq), 1)
            keep = ri >= ci
        for h in range(2):
            s = _dot_nt(qh[h], k)
            if masked:
                s = jnp.where(keep, s, NEG_BIG)
            m_old = m_ref[h]
            m_new = jnp.maximum(m_old, jnp.max(s, axis=-1, keepdims=True))
            alpha = jnp.exp2(m_old - m_new)
            p = jnp.exp2(s - jnp.concatenate([m_new] * (tq // LANES), axis=1))
            acc_ref[h] = (jnp.concatenate([alpha, alpha], axis=1) * acc_ref[h]
                          + _dot(p.astype(BF16), v))
            m_ref[h] = m_new

    def body(kq, carry):
        for u in range(KV_UNROLL):
            block(start + KV_UNROLL * kq + u, False)
        return carry

    full = (i - start) // KV_UNROLL
    lax.fori_loop(0, full, body, 0)
    tail0 = start + full * KV_UNROLL
    for extra in range(KV_UNROLL):
        @pl.when(i - tail0 == extra)
        def _(extra=extra):
            for u in range(extra):
                block(tail0 + u, False)
            block(i, True)

    o = jnp.where(head0, acc_ref[0, :, 0:LANES] / acc_ref[0, :, LANES:2 * LANES],
                  acc_ref[1, :, 0:LANES] / acc_ref[1, :, LANES:2 * LANES])
    o = o * _silu(g_ref[...].astype(F32))
    o_ref[...] = o.astype(o_ref.dtype)


def _fox(proj, f_aug, bsz, seq):
    t = proj.shape[0]
    tq = min(T_ATT, seq)
    nq = seq // tq
    base = COL_FOX // LANES
    nblk = FOX_WIDTH // LANES
    starts = _fox_bounds(proj, f_aug, bsz, seq, tq)
    return pl.pallas_call(
        functools.partial(_fox_kernel, tq=tq, nq=nq),
        grid=(bsz, FOX_PAIRS, nq),
        in_specs=[pl.BlockSpec(memory_space=pltpu.SMEM),
                  pl.BlockSpec((tq, LANES), lambda b, p, i: (b * nq + i, base + p)),
                  pl.BlockSpec((seq, LANES), lambda b, p, i: (b, base + nblk + p)),
                  pl.BlockSpec((seq, LANES), lambda b, p, i: (b, base + 2 * nblk + p)),
                  pl.BlockSpec((tq, LANES), lambda b, p, i: (b * nq + i, base + 3 * nblk + p)),
                  pl.BlockSpec((seq, SMALL_W), lambda b, p, i: (b, 0))],
        out_specs=pl.BlockSpec((tq, LANES), lambda b, p, i: (b * nq + i, p)),
        out_shape=jax.ShapeDtypeStruct((t, FOX_WIDTH), BF16),
        scratch_shapes=[pltpu.VMEM((seq, 2 * LANES), BF16),
                        pltpu.VMEM((seq, 2 * LANES), BF16),
                        pltpu.VMEM((2, tq, LANES), F32),
                        pltpu.VMEM((2, tq, 2 * LANES), F32)],
        compiler_params=_cparams(("parallel", "parallel", "arbitrary")),
        name="fox_attention",
    )(starts, proj, proj, proj, proj, f_aug)


def _out_kernel(ret_ref, ssd_ref, fox_ref, w_ref, x_ref, gate_ref, g_ref, sc_ref, sh_ref,
                *out_refs, emit_x):
    acc = _dot(ret_ref[...], w_ref[0:RET_WIDTH, :])
    acc = acc + _dot(ssd_ref[...], w_ref[RET_WIDTH:RET_WIDTH + SSD_WIDTH, :])
    acc = acc + _dot(fox_ref[...], w_ref[RET_WIDTH + SSD_WIDTH:MIX_WIDTH, :])
    xn = x_ref[...] + gate_ref[0] * acc
    if emit_x:
        out_refs[0][...] = xn
    y_ref = out_refs[-1]
    y_ref[...] = _modnorm(xn, g_ref[...], sc_ref[0], sh_ref[0]).astype(y_ref.dtype)


def _out_projection(ret, ssd, fox, w_out, layer, x2, gate, g, scale, shift, seq, y_dtype, emit_x):
    t, d = x2.shape
    tm = min(TM_OUT, seq)
    per_b = seq // tm
    rowblk = lambda w: pl.BlockSpec((tm, w), lambda i: (i, 0))
    perb = pl.BlockSpec((1, 1, d), lambda i: (i // per_b, 0, 0))
    out_specs = [rowblk(d)]
    out_shape = [jax.ShapeDtypeStruct((t, d), y_dtype)]
    if emit_x:
        out_specs = [rowblk(d)] + out_specs
        out_shape = [jax.ShapeDtypeStruct((t, d), F32)] + out_shape
    return pl.pallas_call(
        functools.partial(_out_kernel, emit_x=emit_x),
        grid=(t // tm,),
        in_specs=[rowblk(RET_WIDTH), rowblk(SSD_WIDTH), rowblk(FOX_WIDTH),
                  pl.BlockSpec((None, MIX_WIDTH, d), lambda i: (layer, 0, 0)),
                  rowblk(d), perb, pl.BlockSpec((1, d), lambda i: (0, 0)), perb, perb],
        out_specs=out_specs,
        out_shape=out_shape,
        compiler_params=_cparams(("parallel",)),
        name="out_proj",
    )(ret, ssd, fox, w_out, x2, gate, g.reshape(1, d), scale, shift)


def _permute_in_weights(w_in):
    o_xbc = 4 * RET_WIDTH
    o_dt = o_xbc + CONV_DIM
    o_z = o_dt + SSD_HEADS
    o_fox = o_z + SSD_WIDTH
    o_f = o_fox + 4 * FOX_WIDTH
    col_scale = np.ones((w_in.shape[-1],), np.float32)
    col_scale[o_fox:o_fox + FOX_WIDTH] = FOX_Q_SCALE
    wb = (w_in * col_scale).astype(BF16)
    main = jnp.concatenate([
        wb[..., 0:o_xbc + SSD_WIDTH],
        wb[..., o_z:o_z + SSD_WIDTH],
        wb[..., o_xbc + SSD_WIDTH:o_dt],
        wb[..., o_fox:o_f],
    ], axis=-1)
    pad = jnp.zeros(wb.shape[:-1] + (SMALL_W - FOX_HEADS - SSD_HEADS,), BF16)
    small = jnp.concatenate([wb[..., o_f:o_f + FOX_HEADS], wb[..., o_dt:o_z], pad], axis=-1)
    return main, small


def _lane_row(depth, pieces):
    row = jnp.zeros((depth, 1, SMALL_W), F32)
    for start, arr in pieces:
        row = row.at[:, 0, start:start + arr.shape[-1]].set(arr.astype(F32))
    return row


def kernel(x, c, positions, norm_g, w_ada, b_ada, w_in, conv_w, conv_b, dt_bias, a_log, d_skip,
           ssd_norm_g, b_forget, w_out, final_g):
    bsz, seq, d = x.shape
    depth = w_in.shape[0]
    t = bsz * seq

    mod = _modulation(c, w_ada, b_ada)
    shift = mod[:, :, 0:d].reshape(depth, bsz, 1, d)
    scale = mod[:, :, d:2 * d].reshape(depth, bsz, 1, d)
    gate = mod[:, :, 2 * d:3 * d].reshape(depth, bsz, 1, d)

    cos_t, sin_t = _rope_tables(positions)
    w_main, w_small = _permute_in_weights(w_in)
    w_out_b = w_out.astype(BF16)
    ret_consts = _retention_consts()
    tril = jnp.asarray(np.tril(np.ones((CHUNK, CHUNK), np.float32)), BF16)
    shifts = _shift_matrices()
    bias_rows = _lane_row(depth, [(LANE_F, b_forget), (LANE_DT, dt_bias)])
    alog_rows = _lane_row(depth, [(LANE_DT, a_log)])
    dskip_rows = jnp.repeat(d_skip, SSD_HEAD_DIM, axis=-1).reshape(depth, 1, SSD_WIDTH)
    zeros_bd = jnp.zeros((bsz, 1, d), F32)

    x2 = x.reshape(t, d)
    h = _first_norm(x2, norm_g[0], scale[0], shift[0], seq)
    out = None
    for l in range(depth):
        proj, small = _in_projection(h, w_main, w_small, l)
        ret = _retention(proj, cos_t, sin_t, ret_consts, bsz, seq)
        ssd, f_rows = _ssd(proj, small, conv_w[l], conv_b[l].reshape(1, CONV_DIM), bias_rows[l],
                           alog_rows[l], dskip_rows[l], ssd_norm_g[l].reshape(1, SSD_WIDTH), tril,
                           shifts, bsz, seq)
        fox = _fox(proj, f_rows, bsz, seq)
        if l + 1 < depth:
            x2, h = _out_projection(ret, ssd, fox, w_out_b, l, x2, gate[l], norm_g[l + 1],
                                    scale[l + 1], shift[l + 1], seq, BF16, True)
        else:
            (out,) = _out_projection(ret, ssd, fox, w_out_b, l, x2, gate[l], final_g,
                                     zeros_bd, zeros_bd, seq, F32, False)
    return out.reshape(bsz, seq, d)
```

```python
import functools
import math

import numpy as np
import jax
import jax.numpy as jnp
from jax import lax
from jax.experimental import pallas as pl
from jax.experimental.pallas import tpu as pltpu

F32 = jnp.float32
BF16 = jnp.bfloat16

RET_HEADS = 4
RET_HEAD_DIM = 128
RET_WIDTH = RET_HEADS * RET_HEAD_DIM
SSD_HEADS = 16
SSD_HEAD_DIM = 64
SSD_WIDTH = SSD_HEADS * SSD_HEAD_DIM
SSD_GROUPS = 2
SSD_STATE = 128
SSD_HEADS_PER_GROUP = SSD_HEADS // SSD_GROUPS
CONV_K = 4
BC_WIDTH = SSD_GROUPS * SSD_STATE
CONV_DIM = SSD_WIDTH + 2 * BC_WIDTH
FOX_HEADS = 8
FOX_HEAD_DIM = 64
FOX_WIDTH = FOX_HEADS * FOX_HEAD_DIM
FOX_PAIRS = FOX_HEADS // 2
MIX_WIDTH = RET_WIDTH + SSD_WIDTH + FOX_WIDTH
CHUNK = 128
ROPE_BASE = 10000.0
EPS = 1e-6
LOG2E = math.log2(math.e)
FOX_Q_SCALE = FOX_HEAD_DIM ** -0.5 * LOG2E

COL_RET = 0
COL_XS = 4 * RET_WIDTH
COL_Z = COL_XS + SSD_WIDTH
COL_B = COL_Z + SSD_WIDTH
COL_C = COL_B + BC_WIDTH
COL_FOX = COL_C + BC_WIDTH
N_MAIN = COL_FOX + 4 * FOX_WIDTH
SMALL_W = 128
LANE_F = 0
LANE_DT = FOX_HEADS

LANES = 128
TM_PROJ = 1024
TN_PROJ = 3328
TM_OUT = 512
TM_NORM = 1024
TB_RET = 1024
TB_SSD = 1024
T_ATT = 512
KV_UNROLL = 4
PRUNE_MARGIN = 152.0
NORM_SLACK = 1.02
VMEM_LIMIT = 48 * 1024 * 1024
NEG_BIG = -1e30


def _cparams(sem):
    return pltpu.CompilerParams(dimension_semantics=sem, vmem_limit_bytes=VMEM_LIMIT)


def _dot(a, b):
    return jnp.dot(a, b, preferred_element_type=F32)


def _dot_nt(a, b):
    return lax.dot_general(a, b, (((1,), (1,)), ((), ())), preferred_element_type=F32)


def _dot_tn(a, b):
    return lax.dot_general(a, b, (((0,), (0,)), ((), ())), preferred_element_type=F32)


def _silu(t):
    return t * jax.nn.sigmoid(t)


def _split3(t):
    hi = t.astype(BF16)
    r1 = t - hi.astype(F32)
    mid = r1.astype(BF16)
    lo = (r1 - mid.astype(F32)).astype(BF16)
    return hi, mid, lo


def _modnorm(t, g, scale, shift):
    var = jnp.mean(t * t, axis=-1, keepdims=True)
    return t * lax.rsqrt(var + EPS) * g * (1.0 + scale) + shift


def _mod_kernel(c_ref, w_ref, b_ref, o_ref):
    a = _silu(c_ref[...])
    w = w_ref[0]
    a_hi = a.astype(BF16)
    a_lo = (a - a_hi.astype(F32)).astype(BF16)
    w_hi = w.astype(BF16)
    w_lo = (w - w_hi.astype(F32)).astype(BF16)
    o_ref[0] = _dot(a_hi, w_hi) + _dot(a_hi, w_lo) + _dot(a_lo, w_hi) + b_ref[0]


def _modulation(c, w_ada, b_ada):
    depth, d, n3 = w_ada.shape
    bsz = c.shape[0]
    rows = 16
    c_pad = jnp.zeros((rows, d), F32).at[:bsz].set(c)
    tn = 1024
    out = pl.pallas_call(
        _mod_kernel,
        grid=(depth, n3 // tn),
        in_specs=[pl.BlockSpec((rows, d), lambda l, j: (0, 0)),
                  pl.BlockSpec((1, d, tn), lambda l, j: (l, 0, j)),
                  pl.BlockSpec((1, 1, tn), lambda l, j: (l, 0, j))],
        out_specs=pl.BlockSpec((1, rows, tn), lambda l, j: (l, 0, j)),
        out_shape=jax.ShapeDtypeStruct((depth, rows, n3), F32),
        compiler_params=_cparams(("parallel", "parallel")),
        name="adaln_mod",
    )(c_pad, w_ada, b_ada.reshape(depth, 1, n3))
    return out[:, :bsz]


def _rope_kernel(pos_ref, freq_ref, cos_ref, sin_ref):
    pos = pos_ref[0].astype(F32)
    ang = freq_ref[...] * pos
    c = jnp.cos(ang)
    s = jnp.sin(ang)
    cos_ref[...] = jnp.concatenate([c, c], axis=0).T
    sin_ref[...] = jnp.concatenate([-s, s], axis=0).T


def _rope_tables(positions):
    bsz, seq = positions.shape
    half = RET_HEAD_DIM // 2
    freq = (ROPE_BASE ** (-jnp.arange(half, dtype=F32) / half)).reshape(half, 1)
    tm = 512
    nt = seq // tm
    shp = jax.ShapeDtypeStruct((bsz * seq, RET_HEAD_DIM), F32)
    return pl.pallas_call(
        _rope_kernel,
        grid=(bsz, nt),
        in_specs=[pl.BlockSpec((1, 1, tm), lambda b, i: (b, 0, i)),
                  pl.BlockSpec((half, 1), lambda b, i: (0, 0))],
        out_specs=[pl.BlockSpec((tm, RET_HEAD_DIM), lambda b, i: (b * nt + i, 0)),
                   pl.BlockSpec((tm, RET_HEAD_DIM), lambda b, i: (b * nt + i, 0))],
        out_shape=[shp, shp],
        compiler_params=_cparams(("parallel", "parallel")),
        name="rope_tables",
    )(positions.reshape(bsz, 1, seq), freq)


def _norm_kernel(x_ref, g_ref, sc_ref, sh_ref, h_ref):
    h_ref[...] = _modnorm(x_ref[...], g_ref[...], sc_ref[0], sh_ref[0]).astype(h_ref.dtype)


def _first_norm(x2, g, scale, shift, seq):
    t, d = x2.shape
    tm = min(TM_NORM, seq)
    per_b = seq // tm
    return pl.pallas_call(
        _norm_kernel,
        grid=(t // tm,),
        in_specs=[pl.BlockSpec((tm, d), lambda i: (i, 0)),
                  pl.BlockSpec((1, d), lambda i: (0, 0)),
                  pl.BlockSpec((1, 1, d), lambda i: (i // per_b, 0, 0)),
                  pl.BlockSpec((1, 1, d), lambda i: (i // per_b, 0, 0))],
        out_specs=pl.BlockSpec((tm, d), lambda i: (i, 0)),
        out_shape=jax.ShapeDtypeStruct((t, d), BF16),
        compiler_params=_cparams(("parallel",)),
        name="first_norm",
    )(x2, g.reshape(1, d), scale, shift)


def _inproj_kernel(h_ref, w_ref, ws_ref, o_ref, os_ref):
    h = h_ref[...]
    o_ref[...] = _dot(h, w_ref[...]).astype(o_ref.dtype)

    @pl.when(pl.program_id(1) == 0)
    def _():
        os_ref[...] = _dot(h, ws_ref[...])


def _in_projection(h, w_main, w_small, layer):
    t, d = h.shape
    tm = min(TM_PROJ, t)
    return pl.pallas_call(
        _inproj_kernel,
        grid=(t // tm, N_MAIN // TN_PROJ),
        in_specs=[pl.BlockSpec((tm, d), lambda i, j: (i, 0)),
                  pl.BlockSpec((None, d, TN_PROJ), lambda i, j: (layer, 0, j)),
                  pl.BlockSpec((None, d, SMALL_W), lambda i, j: (layer, 0, 0))],
        out_specs=[pl.BlockSpec((tm, TN_PROJ), lambda i, j: (i, j)),
                   pl.BlockSpec((tm, SMALL_W), lambda i, j: (i, 0))],
        out_shape=[jax.ShapeDtypeStruct((t, N_MAIN), BF16),
                   jax.ShapeDtypeStruct((t, SMALL_W), F32)],
        compiler_params=_cparams(("parallel", "arbitrary")),
        name="in_proj",
    )(h, w_main, w_small)


def _retention_consts():
    h = np.arange(RET_HEADS, dtype=np.float64)
    log_g = np.log(1.0 - 2.0 ** (-5.0 - h))
    idx = np.arange(CHUNK, dtype=np.float64)
    diff = idx[:, None] - idx[None, :]
    intra = np.where(diff >= 0, np.exp(log_g[:, None, None] * np.maximum(diff, 0.0)), 0.0)
    dq = np.exp(log_g[:, None] * (idx + 1.0))
    dk = np.exp(log_g[:, None] * (CHUNK - 1.0 - idx))
    dq = np.broadcast_to(dq[:, :, None], (RET_HEADS, CHUNK, RET_HEAD_DIM))
    dk = np.broadcast_to(dk[:, :, None], (RET_HEADS, CHUNK, RET_HEAD_DIM))
    dchunk = tuple(float(v) for v in np.exp(log_g * CHUNK))
    return (jnp.asarray(intra, F32), jnp.asarray(dq, F32), jnp.asarray(dk, F32), dchunk)


def _ret_kernel(q_ref, k_ref, v_ref, g_ref, cos_ref, sin_ref, di_ref, dq_ref, dk_ref,
                o_ref, s_ref, *, nchunk, dchunk):
    @pl.when(pl.program_id(1) == 0)
    def _():
        s_ref[...] = jnp.zeros_like(s_ref)

    kscale = RET_HEAD_DIM ** -0.5

    def body(c, carry):
        rows = pl.ds(pl.multiple_of(c * CHUNK, CHUNK), CHUNK)
        cs = cos_ref[rows, :]
        sn = sin_ref[rows, :]
        for h in range(RET_HEADS):
            cols = slice(h * RET_HEAD_DIM, (h + 1) * RET_HEAD_DIM)
            q = q_ref[rows, cols].astype(F32)
            k = k_ref[rows, cols].astype(F32)
            v = v_ref[rows, cols]
            qr = q * cs + pltpu.roll(q, RET_HEAD_DIM // 2, 1) * sn
            kr = (k * cs + pltpu.roll(k, RET_HEAD_DIM // 2, 1) * sn) * kscale
            qb = qr.astype(BF16)
            kb = kr.astype(BF16)
            state = s_ref[h]
            s = _dot_nt(qb, kb) * di_ref[h]
            o = _dot(s.astype(BF16), v) + _dot(qb, state.astype(BF16)) * dq_ref[h]
            kd = (kr * dk_ref[h]).astype(BF16)
            s_ref[h] = state * dchunk[h] + _dot_tn(kd, v)
            o = o * lax.rsqrt(jnp.mean(o * o, axis=-1, keepdims=True) + EPS)
            o = o * _silu(g_ref[rows, cols].astype(F32))
            o_ref[rows, cols] = o.astype(o_ref.dtype)
        return carry

    lax.fori_loop(0, nchunk, body, 0, unroll=4)


def _retention(proj, cos_t, sin_t, consts, bsz, seq):
    t = proj.shape[0]
    tb = min(TB_RET, seq)
    per_b = seq // tb
    intra, dq, dk, dchunk = consts
    row = lambda b, j: b * per_b + j
    cblk = lambda idx: pl.BlockSpec((tb, RET_WIDTH), lambda b, j: (row(b, j), idx))
    const3 = pl.BlockSpec((RET_HEADS, CHUNK, RET_HEAD_DIM), lambda b, j: (0, 0, 0))
    tblk = pl.BlockSpec((tb, RET_HEAD_DIM), lambda b, j: (row(b, j), 0))
    base = COL_RET // RET_WIDTH
    return pl.pallas_call(
        functools.partial(_ret_kernel, nchunk=tb // CHUNK, dchunk=dchunk),
        grid=(bsz, per_b),
        in_specs=[cblk(base), cblk(base + 1), cblk(base + 2), cblk(base + 3),
                  tblk, tblk, const3, const3, const3],
        out_specs=pl.BlockSpec((tb, RET_WIDTH), lambda b, j: (row(b, j), 0)),
        out_shape=jax.ShapeDtypeStruct((t, RET_WIDTH), BF16),
        scratch_shapes=[pltpu.VMEM((RET_HEADS, RET_HEAD_DIM, RET_HEAD_DIM), F32)],
        compiler_params=_cparams(("parallel", "arbitrary")),
        name="retention",
    )(proj, proj, proj, proj, cos_t, sin_t, intra, dq, dk)


def _ssd_kernel(xs_ref, z_ref, b_ref, c_ref, sm_ref, cw_ref, cb_ref, bias_ref, alog_ref,
                dskip_ref, ng_ref, tril_ref, shift_ref, y_ref, f_ref,
                ubuf, st_ref, ybuf2, fcar, xcf2, xsb2, inter2, *, tb):
    j = pl.program_id(1)

    @pl.when(j == 0)
    def _():
        ubuf[0:CHUNK, :] = jnp.zeros((CHUNK, CONV_DIM), BF16)
        st_ref[...] = jnp.zeros_like(st_ref)
        fcar[...] = jnp.zeros_like(fcar)

    @pl.when(j > 0)
    def _():
        ubuf[0:CHUNK, :] = ubuf[tb:tb + CHUNK, :]

    ubuf[CHUNK:CHUNK + tb, 0:SSD_WIDTH] = xs_ref[...]
    ubuf[CHUNK:CHUNK + tb, SSD_WIDTH:SSD_WIDTH + BC_WIDTH] = b_ref[...]
    ubuf[CHUNK:CHUNK + tb, SSD_WIDTH + BC_WIDTH:CONV_DIM] = c_ref[...]

    slab = 2 * LANES

    def conv_silu(c, xcf, xsb):
        for s0 in range(0, CONV_DIM, slab):
            cols = slice(s0, s0 + slab)
            win = ubuf[c * CHUNK:(c + 2) * CHUNK, cols]
            acc = cb_ref[:, cols] + cw_ref[CONV_K - 1:CONV_K, cols] * win[CHUNK:2 * CHUNK, :].astype(F32)
            for sh in range(1, CONV_K):
                tap = CONV_K - 1 - sh
                acc = acc + cw_ref[tap:tap + 1, cols] * _dot(shift_ref[sh - 1], win)
            xc = _silu(acc)
            xcf[:, cols] = xc
            if s0 < SSD_WIDTH:
                xsb[:, cols] = xc.astype(BF16)

    lane = lax.broadcasted_iota(jnp.int32, (1, LANES), 1)
    is_f = lane < LANE_DT
    half_lane = lane < SSD_HEAD_DIM
    a_row = jnp.where((lane >= LANE_DT) & (lane < LANE_DT + SSD_HEADS), -jnp.exp(alog_ref[...]), 0.0)
    ri = lax.broadcasted_iota(jnp.int32, (CHUNK, CHUNK), 0)
    ci = lax.broadcasted_iota(jnp.int32, (CHUNK, CHUNK), 1)
    causal = ri >= ci
    tril = tril_ref[...]

    for c in range(tb // CHUNK):
        r0 = c * CHUNK
        xcf, xsb, ybuf, inter = xcf2.at[c % 2], xsb2.at[c % 2], ybuf2.at[c % 2], inter2.at[c % 2]
        conv_silu(c, xcf, xsb)

        pre = sm_ref[r0:r0 + CHUNK, :] + bias_ref[...]
        tail_term = jnp.log1p(jnp.exp(-jnp.abs(pre)))
        sp = jnp.maximum(pre, 0.0) + tail_term
        val = jnp.where(is_f, -(jnp.maximum(-pre, 0.0) + tail_term), sp * a_row)
        v_hi, v_mid, v_lo = _split3(val)
        cum = _dot(tril, v_hi) + _dot(tril, v_mid) + _dot(tril, v_lo)
        cum = cum + fcar[...]
        fcar[...] = jnp.where(is_f, cum[CHUNK - 1:CHUNK, :], 0.0)
        cum_t = cum.T
        dt_t = sp.T
        n_hi, n_mid, n_lo = _split3(jnp.where(is_f, cum * -LOG2E, 0.0))
        f_ref[r0:r0 + CHUNK, :] = (n_hi.astype(F32) + pltpu.roll(n_mid.astype(F32), FOX_HEADS, 1)
                                   + pltpu.roll(n_lo.astype(F32), 2 * FOX_HEADS, 1)).astype(BF16)
        e_col = jnp.exp(cum)

        for g in range(SSD_GROUPS):
            bcol = SSD_WIDTH + g * SSD_STATE
            bm = xcf[:, bcol:bcol + SSD_STATE]
            cm_b = xcf[:, bcol + BC_WIDTH:bcol + BC_WIDTH + SSD_STATE].astype(BF16)
            cb = _dot_nt(cm_b, bm.astype(BF16))
            bm_t = bm.T
            inter[...] = _dot(cm_b, st_ref[g].astype(BF16))
            for pr in range(SSD_HEADS_PER_GROUP // 2):
                gp = g * (SSD_HEADS_PER_GROUP // 2) + pr
                xp_b = xsb[:, gp * LANES:(gp + 1) * LANES]
                ys = []
                ups = []
                ecs = []
                els = []
                for hh in range(2):
                    ln = LANE_DT + 2 * gp + hh
                    a_c = cum[:, ln:ln + 1]
                    a_r = cum_t[ln:ln + 1, :]
                    d_r = dt_t[ln:ln + 1, :]
                    lm = jnp.where(causal, jnp.exp(a_c - a_r), 0.0)
                    m = cb * lm * d_r
                    ys.append(_dot(m.astype(BF16), xp_b))
                    last = a_r[:, CHUNK - 1:CHUNK]
                    w_r = d_r * jnp.exp(last - a_r)
                    ups.append(_dot((bm_t * w_r).astype(BF16), xp_b))
                    ecs.append(e_col[:, ln:ln + 1])
                    els.append(jnp.exp(last))
                lanes = slice(pr * LANES, (pr + 1) * LANES)
                y_pair = (jnp.where(half_lane, ys[0], ys[1])
                          + jnp.where(half_lane, ecs[0], ecs[1]) * inter[:, lanes])
                ybuf[:, gp * LANES:(gp + 1) * LANES] = y_pair
                st_ref[g, :, lanes] = (st_ref[g, :, lanes] * jnp.where(half_lane, els[0], els[1])
                                       + jnp.where(half_lane, ups[0], ups[1]))

        ss = jnp.zeros((CHUNK, 1), F32)
        for s0 in range(0, SSD_WIDTH, slab):
            cols = slice(s0, s0 + slab)
            gated = ((ybuf[:, cols] + dskip_ref[:, cols] * xcf[:, cols])
                     * _silu(z_ref[r0:r0 + CHUNK, cols].astype(F32)))
            ybuf[:, cols] = gated
            ss = ss + jnp.sum(gated * gated, axis=-1, keepdims=True)
        inv = lax.rsqrt(ss * (1.0 / SSD_WIDTH) + EPS)
        for s0 in range(0, SSD_WIDTH, slab):
            cols = slice(s0, s0 + slab)
            y_ref[r0:r0 + CHUNK, cols] = (ybuf[:, cols] * inv * ng_ref[:, cols]).astype(y_ref.dtype)


def _shift_matrices():
    m = np.zeros((CONV_K - 1, CHUNK, 2 * CHUNK), np.float32)
    t = np.arange(CHUNK)
    for sh in range(1, CONV_K):
        m[sh - 1, t, CHUNK + t - sh] = 1.0
    return jnp.asarray(m, BF16)


def _ssd(proj, small, conv_w, conv_b, bias_row, alog_row, dskip_row, norm_g, tril, shifts, bsz, seq):
    t = proj.shape[0]
    tb = min(TB_SSD, seq)
    per_b = seq // tb
    row = lambda b, j: b * per_b + j
    full = lambda shape: pl.BlockSpec(shape, lambda b, j: (0,) * len(shape))
    return pl.pallas_call(
        functools.partial(_ssd_kernel, tb=tb),
        grid=(bsz, per_b),
        in_specs=[pl.BlockSpec((tb, SSD_WIDTH), lambda b, j: (row(b, j), COL_XS // SSD_WIDTH)),
                  pl.BlockSpec((tb, SSD_WIDTH), lambda b, j: (row(b, j), COL_Z // SSD_WIDTH)),
                  pl.BlockSpec((tb, BC_WIDTH), lambda b, j: (row(b, j), COL_B // BC_WIDTH)),
                  pl.BlockSpec((tb, BC_WIDTH), lambda b, j: (row(b, j), COL_C // BC_WIDTH)),
                  pl.BlockSpec((tb, SMALL_W), lambda b, j: (row(b, j), 0)),
                  full((CONV_K, CONV_DIM)), full((1, CONV_DIM)), full((1, SMALL_W)),
                  full((1, SMALL_W)), full((1, SSD_WIDTH)), full((1, SSD_WIDTH)),
                  full((CHUNK, CHUNK)), full((CONV_K - 1, CHUNK, 2 * CHUNK))],
        out_specs=[pl.BlockSpec((tb, SSD_WIDTH), lambda b, j: (row(b, j), 0)),
                   pl.BlockSpec((tb, SMALL_W), lambda b, j: (row(b, j), 0))],
        out_shape=[jax.ShapeDtypeStruct((t, SSD_WIDTH), BF16),
                   jax.ShapeDtypeStruct((t, SMALL_W), BF16)],
        scratch_shapes=[pltpu.VMEM((CHUNK + tb, CONV_DIM), BF16),
                        pltpu.VMEM((SSD_GROUPS, SSD_STATE, SSD_HEADS_PER_GROUP * SSD_HEAD_DIM), F32),
                        pltpu.VMEM((2, CHUNK, SSD_WIDTH), F32),
                        pltpu.VMEM((1, SMALL_W), F32),
                        pltpu.VMEM((2, CHUNK, CONV_DIM), F32),
                        pltpu.VMEM((2, CHUNK, SSD_WIDTH), BF16),
                        pltpu.VMEM((2, CHUNK, SSD_HEADS_PER_GROUP * SSD_HEAD_DIM), F32)],
        compiler_params=_cparams(("parallel", "arbitrary")),
        name="ssd",
    )(proj, proj, proj, proj, small, conv_w, conv_b, bias_row, alog_row, dskip_row, norm_g, tril,
      shifts)


def _fox_bounds_kernel(q_ref, k_ref, nf_ref, nft_ref, o_ref, *, tq, nq):
    pair = pl.program_id(1)
    rows = 16
    hrow = lax.broadcasted_iota(jnp.int32, (rows, LANES), 0)
    hlane = lax.broadcasted_iota(jnp.int32, (rows, LANES), 1)
    selector = jnp.where((hrow < 2) & ((hlane >= FOX_HEAD_DIM) == (hrow == 1)), 1.0, 0.0).astype(BF16)
    q = q_ref[...]
    k = k_ref[...]
    qn2 = _dot_nt(selector, q * q)
    kn2 = jnp.max(_dot_nt(selector, k * k), axis=1, keepdims=True)
    diag = _dot_nt(selector, q * k)
    qmax2 = jnp.zeros((rows, LANES), F32)
    dmin = jnp.zeros((rows, LANES), F32)
    for t in range(nq):
        tile = slice(t * tq, (t + 1) * tq)
        qmax2 = jnp.where(hlane == t, jnp.max(qn2[:, tile], axis=1, keepdims=True), qmax2)
        dmin = jnp.where(hlane == t, jnp.min(diag[:, tile], axis=1, keepdims=True), dmin)
    qk = jnp.sqrt(qmax2 * kn2) * NORM_SLACK

    nft = nft_ref[0]
    nft_prev = pltpu.roll(nft, 1, 1)
    nf = nf_ref[0]
    sub8 = lax.broadcasted_iota(jnp.int32, (FOX_HEADS, LANES), 0)
    lane8 = lax.broadcasted_iota(jnp.int32, (nq, FOX_HEADS), 1)
    kb_idx = lax.broadcasted_iota(jnp.int32, (nq, LANES), 0)
    t_idx = lax.broadcasted_iota(jnp.int32, (nq, LANES), 1)
    first = None
    for h in range(2):
        hd = 2 * pair + h
        prev_row = jnp.sum(jnp.where(sub8 == hd, nft_prev, 0.0), axis=0, keepdims=True)
        nf_col = jnp.sum(jnp.where(lane8 == hd, nf, 0.0), axis=1, keepdims=True)
        thresh = prev_row + dmin[h:h + 1, :] - NORM_SLACK * qk[h:h + 1, :] - PRUNE_MARGIN
        dead = (nf_col <= thresh) & (kb_idx < t_idx)
        count = jnp.sum(jnp.where(dead, 1.0, 0.0), axis=0, keepdims=True)
        first = count if first is None else jnp.minimum(first, count)
    o_ref[0, 0] = jnp.broadcast_to(first, (FOX_HEADS, LANES)).astype(jnp.int32)


def _fox_bounds(proj, f_aug, bsz, seq, tq):
    nq = seq // tq
    base = COL_FOX // LANES
    nblk = FOX_WIDTH // LANES
    ends = f_aug.reshape(bsz, nq, tq, SMALL_W)[:, :, tq - 1, 0:3 * FOX_HEADS].astype(F32)
    nf = ends.reshape(bsz, nq, 3, FOX_HEADS).sum(axis=2)
    nft = jnp.full((bsz, FOX_HEADS, LANES), jnp.inf, F32).at[:, :, 0:nq].set(nf.transpose(0, 2, 1))
    out = pl.pallas_call(
        functools.partial(_fox_bounds_kernel, tq=tq, nq=nq),
        grid=(bsz, FOX_PAIRS),
        in_specs=[pl.BlockSpec((seq, LANES), lambda b, p: (b, base + p)),
                  pl.BlockSpec((seq, LANES), lambda b, p: (b, base + nblk + p)),
                  pl.BlockSpec((1, nq, FOX_HEADS), lambda b, p: (b, 0, 0)),
                  pl.BlockSpec((1, FOX_HEADS, LANES), lambda b, p: (b, 0, 0))],
        out_specs=pl.BlockSpec((1, 1, FOX_HEADS, LANES), lambda b, p: (b, p, 0, 0)),
        out_shape=jax.ShapeDtypeStruct((bsz, FOX_PAIRS, FOX_HEADS, LANES), jnp.int32),
        compiler_params=_cparams(("parallel", "parallel")),
        name="fox_bounds",
    )(proj, proj, nf, nft)
    return out[:, :, 0, 0:nq].reshape(-1)


def _fox_kernel(start_ref, q_ref, k_ref, v_ref, g_ref, f_ref, o_ref, kaug, vaug, m_ref, acc_ref,
                *, tq, nq):
    bat = pl.program_id(0)
    pair = pl.program_id(1)
    i = pl.program_id(2)
    lane = lax.broadcasted_iota(jnp.int32, (1, LANES), 1)
    head0 = lane < FOX_HEAD_DIM

    @pl.when(i == 0)
    def _():
        kaug[:, 0:LANES] = k_ref[...]
        kaug[:, LANES:2 * LANES] = f_ref[...]
        vaug[:, 0:LANES] = v_ref[...]
        vaug[:, LANES:2 * LANES] = jnp.ones((vaug.shape[0], LANES), BF16)

    q = q_ref[...]
    zero = jnp.zeros_like(q)
    qh = []
    for h in range(2):
        hd = 2 * pair + h
        sel = (lane == hd) | (lane == hd + FOX_HEADS) | (lane == hd + 2 * FOX_HEADS)
        ones = jnp.broadcast_to(jnp.where(sel, 1.0, 0.0).astype(BF16), (tq, LANES))
        qm = jnp.where(head0, q, zero) if h == 0 else jnp.where(head0, zero, q)
        qh.append(jnp.concatenate([qm, ones], axis=1))
    start = start_ref[(bat * FOX_PAIRS + pair) * nq + i]

    m_ref[...] = jnp.full(m_ref.shape, NEG_BIG, F32)
    acc_ref[...] = jnp.zeros_like(acc_ref)

    def block(kb, masked):
        cols = pl.ds(pl.multiple_of(kb * tq, tq), tq)
        k = kaug[cols, :]
        v = vaug[cols, :]
        if masked:
            ri = lax.broadcasted_iota(jnp.int32, (tq, tq), 0)
            ci = lax.broadcasted_iota(jnp.int32, (tq, tq), 1)
            keep = ri >= ci
        half = tq // 2
        for h in range(2):
            for r0 in (0, half):
                rows = slice(r0, r0 + half)
                s = _dot_nt(qh[h][rows], k)
                if masked:
                    s = jnp.where(keep[rows], s, NEG_BIG)
                m_old = m_ref[h, rows, :]
                m_new = jnp.maximum(m_old, jnp.max(s, axis=-1, keepdims=True))
                alpha = jnp.exp2(m_old - m_new)
                p = jnp.exp2(s - jnp.concatenate([m_new] * (tq // LANES), axis=1))
                acc_ref[h, rows, :] = (jnp.concatenate([alpha, alpha], axis=1) * acc_ref[h, rows, :]
                                       + _dot(p.astype(BF16), v))
                m_ref[h, rows, :] = m_new

    def body(kq, carry):
        for u in range(KV_UNROLL):
            block(start + KV_UNROLL * kq + u, False)
        return carry

    full = (i - start) // KV_UNROLL
    lax.fori_loop(0, full, body, 0)
    tail0 = start + full * KV_UNROLL
    for extra in range(KV_UNROLL):
        @pl.when(i - tail0 == extra)
        def _(extra=extra):
            for u in range(extra):
                block(tail0 + u, False)
            block(i, True)

    o = jnp.where(head0, acc_ref[0, :, 0:LANES] / acc_ref[0, :, LANES:2 * LANES],
                  acc_ref[1, :, 0:LANES] / acc_ref[1, :, LANES:2 * LANES])
    o = o * _silu(g_ref[...].astype(F32))
    o_ref[...] = o.astype(o_ref.dtype)


def _fox(proj, f_aug, bsz, seq):
    t = proj.shape[0]
    tq = min(T_ATT, seq)
    nq = seq // tq
    base = COL_FOX // LANES
    nblk = FOX_WIDTH // LANES
    starts = _fox_bounds(proj, f_aug, bsz, seq, tq)
    return pl.pallas_call(
        functools.partial(_fox_kernel, tq=tq, nq=nq),
        grid=(bsz, FOX_PAIRS, nq),
        in_specs=[pl.BlockSpec(memory_space=pltpu.SMEM),
                  pl.BlockSpec((tq, LANES), lambda b, p, i: (b * nq + i, base + p)),
                  pl.BlockSpec((seq, LANES), lambda b, p, i: (b, base + nblk + p)),
                  pl.BlockSpec((seq, LANES), lambda b, p, i: (b, base + 2 * nblk + p)),
                  pl.BlockSpec((tq, LANES), lambda b, p, i: (b * nq + i, base + 3 * nblk + p)),
                  pl.BlockSpec((seq, SMALL_W), lambda b, p, i: (b, 0))],
        out_specs=pl.BlockSpec((tq, LANES), lambda b, p, i: (b * nq + i, p)),
        out_shape=jax.ShapeDtypeStruct((t, FOX_WIDTH), BF16),
        scratch_shapes=[pltpu.VMEM((seq, 2 * LANES), BF16),
                        pltpu.VMEM((seq, 2 * LANES), BF16),
                        pltpu.VMEM((2, tq, LANES), F32),
                        pltpu.VMEM((2, tq, 2 * LANES), F32)],
        compiler_params=_cparams(("parallel", "parallel", "arbitrary")),
        name="fox_attention",
    )(starts, proj, proj, proj, proj, f_aug)


def _out_kernel(ret_ref, ssd_ref, fox_ref, w_ref, x_ref, gate_ref, g_ref, sc_ref, sh_ref,
                *out_refs, emit_x):
    acc = _dot(ret_ref[...], w_ref[0:RET_WIDTH, :])
    acc = acc + _dot(ssd_ref[...], w_ref[RET_WIDTH:RET_WIDTH + SSD_WIDTH, :])
    acc = acc + _dot(fox_ref[...], w_ref[RET_WIDTH + SSD_WIDTH:MIX_WIDTH, :])
    xn = x_ref[...] + gate_ref[0] * acc
    if emit_x:
        out_refs[0][...] = xn
    y_ref = out_refs[-1]
    y_ref[...] = _modnorm(xn, g_ref[...], sc_ref[0], sh_ref[0]).astype(y_ref.dtype)


def _out_projection(ret, ssd, fox, w_out, layer, x2, gate, g, scale, shift, seq, y_dtype, emit_x):
    t, d = x2.shape
    tm = min(TM_OUT, seq)
    per_b = seq // tm
    rowblk = lambda w: pl.BlockSpec((tm, w), lambda i: (i, 0))
    perb = pl.BlockSpec((1, 1, d), lambda i: (i // per_b, 0, 0))
    out_specs = [rowblk(d)]
    out_shape = [jax.ShapeDtypeStruct((t, d), y_dtype)]
    if emit_x:
        out_specs = [rowblk(d)] + out_specs
        out_shape = [jax.ShapeDtypeStruct((t, d), F32)] + out_shape
    return pl.pallas_call(
        functools.partial(_out_kernel, emit_x=emit_x),
        grid=(t // tm,),
        in_specs=[rowblk(RET_WIDTH), rowblk(SSD_WIDTH), rowblk(FOX_WIDTH),
                  pl.BlockSpec((None, MIX_WIDTH, d), lambda i: (layer, 0, 0)),
                  rowblk(d), perb, pl.BlockSpec((1, d), lambda i: (0, 0)), perb, perb],
        out_specs=out_specs,
        out_shape=out_shape,
        compiler_params=_cparams(("parallel",)),
        name="out_proj",
    )(ret, ssd, fox, w_out, x2, gate, g.reshape(1, d), scale, shift)


def _permute_in_weights(w_in):
    o_xbc = 4 * RET_WIDTH
    o_dt = o_xbc + CONV_DIM
    o_z = o_dt + SSD_HEADS
    o_fox = o_z + SSD_WIDTH
    o_f = o_fox + 4 * FOX_WIDTH
    col_scale = np.ones((w_in.shape[-1],), np.float32)
    col_scale[o_fox:o_fox + FOX_WIDTH] = FOX_Q_SCALE
    wb = (w_in * col_scale).astype(BF16)
    main = jnp.concatenate([
        wb[..., 0:o_xbc + SSD_WIDTH],
        wb[..., o_z:o_z + SSD_WIDTH],
        wb[..., o_xbc + SSD_WIDTH:o_dt],
        wb[..., o_fox:o_f],
    ], axis=-1)
    pad = jnp.zeros(wb.shape[:-1] + (SMALL_W - FOX_HEADS - SSD_HEADS,), BF16)
    small = jnp.concatenate([wb[..., o_f:o_f + FOX_HEADS], wb[..., o_dt:o_z], pad], axis=-1)
    return main, small


def _lane_row(depth, pieces):
    row = jnp.zeros((depth, 1, SMALL_W), F32)
    for start, arr in pieces:
        row = row.at[:, 0, start:start + arr.shape[-1]].set(arr.astype(F32))
    return row


def kernel(x, c, positions, norm_g, w_ada, b_ada, w_in, conv_w, conv_b, dt_bias, a_log, d_skip,
           ssd_norm_g, b_forget, w_out, final_g):
    bsz, seq, d = x.shape
    depth = w_in.shape[0]
    t = bsz * seq

    mod = _modulation(c, w_ada, b_ada)
    shift = mod[:, :, 0:d].reshape(depth, bsz, 1, d)
    scale = mod[:, :, d:2 * d].reshape(depth, bsz, 1, d)
    gate = mod[:, :, 2 * d:3 * d].reshape(depth, bsz, 1, d)

    cos_t, sin_t = _rope_tables(positions)
    w_main, w_small = _permute_in_weights(w_in)
    w_out_b = w_out.astype(BF16)
    ret_consts = _retention_consts()
    tril = jnp.asarray(np.tril(np.ones((CHUNK, CHUNK), np.float32)), BF16)
    shifts = _shift_matrices()
    bias_rows = _lane_row(depth, [(LANE_F, b_forget), (LANE_DT, dt_bias)])
    alog_rows = _lane_row(depth, [(LANE_DT, a_log)])
    dskip_rows = jnp.repeat(d_skip, SSD_HEAD_DIM, axis=-1).reshape(depth, 1, SSD_WIDTH)
    zeros_bd = jnp.zeros((bsz, 1, d), F32)

    x2 = x.reshape(t, d)
    h = _first_norm(x2, norm_g[0], scale[0], shift[0], seq)
    out = None
    for l in range(depth):
        proj, small = _in_projection(h, w_main, w_small, l)
        ret = _retention(proj, cos_t, sin_t, ret_consts, bsz, seq)
        ssd, f_rows = _ssd(proj, small, conv_w[l], conv_b[l].reshape(1, CONV_DIM), bias_rows[l],
                           alog_rows[l], dskip_rows[l], ssd_norm_g[l].reshape(1, SSD_WIDTH), tril,
                           shifts, bsz, seq)
        fox = _fox(proj, f_rows, bsz, seq)
        if l + 1 < depth:
            x2, h = _out_projection(ret, ssd, fox, w_out_b, l, x2, gate[l], norm_g[l + 1],
                                    scale[l + 1], shift[l + 1], seq, BF16, True)
        else:
            (out,) = _out_projection(ret, ssd, fox, w_out_b, l, x2, gate[l], final_g,
                                     zeros_bd, zeros_bd, seq, F32, False)
    return out.reshape(bsz, seq, d)
```
